```python
import math
import jax
import jax.numpy as jnp
from jax import lax
import numpy as np

D_MODEL = 4096
BATCH = 4
SEQ = 4096
DEPTH = 1
DEC_BATCH = 2
DEC_SEQ = 4096
PAST_LEN = 128

LRU_WIDTH = D_MODEL // 2
LRU_BLOCKS = 8
LRU_BLOCK = LRU_WIDTH // LRU_BLOCKS
CONV_WIDTH = 4
CONV_LEFT = 2
LRU_C = 8.0
LRU_MIN_RAD = 0.9
LRU_MAX_RAD = 0.999
ATT_WIDTH = D_MODEL - LRU_WIDTH
N_HEADS = 8
HEAD_DIM = ATT_WIDTH // N_HEADS // 2
V_DIM = 2 * HEAD_DIM
ROT_DIM = HEAD_DIM // 4
ROPE_THETA = 500000.0
Q_BLOCK = 128
IN_COLS = 2 * LRU_WIDTH + 3 * ATT_WIDTH
N_EXPERTS = 16
EC_CAPACITY_FACTOR = 2
EXPERT_FF = D_MODEL // 2
NORM_EPS = 1e-6

kernel_name = 'hymba_rglru_diffattn_ec_moe_encoder'


def rmsnorm(x, g):
    xf = x.astype(jnp.float32)
    y = xf * lax.rsqrt(jnp.mean(xf * xf, axis=-1, keepdims=True) + NORM_EPS)
    return (y * g.astype(jnp.float32)).astype(x.dtype)


def centred_depthwise_conv(x, w, b):
    S = x.shape[1]
    xp = jnp.pad(x, ((0, 0), (CONV_LEFT, CONV_WIDTH - 1 - CONV_LEFT), (0, 0)))
    y = xp[:, 0:S] * w[0]
    for k in range(1, CONV_WIDTH):
        y = y + xp[:, k:k + S] * w[k]
    return y + b


def rg_lru_scan(xc, wa, ba, wi, bi, L):
    B, S, W = xc.shape
    xb = xc.reshape(B, S, LRU_BLOCKS, LRU_BLOCK)
    r = jax.nn.sigmoid(jnp.einsum('bsnc,ncd->bsnd', xb, wa.astype(jnp.float32)).reshape(B, S, W) + ba.astype(jnp.float32))
    i = jax.nn.sigmoid(jnp.einsum('bsnc,ncd->bsnd', xb, wi.astype(jnp.float32)).reshape(B, S, W) + bi.astype(jnp.float32))
    log_a = -LRU_C * r * jax.nn.softplus(-L.astype(jnp.float32))
    a = jnp.exp(log_a)
    b = jnp.sqrt(-jnp.expm1(2.0 * log_a)) * (i * xc)

    def combine(left, right):
        a_l, b_l = left
        a_r, b_r = right
        return a_l * a_r, a_r * b_l + b_r

    _, h = lax.associative_scan(combine, (a, b), axis=1)
    return h


def recurrent_group(xb, gb, conv_w, conv_b, wa, ba, wi, bi, L, norm_g):
    xc = centred_depthwise_conv(xb, conv_w, conv_b).astype(jnp.float32)
    h_fwd = rg_lru_scan(xc, wa[0], ba[0], wi[0], bi[0], L[0])
    h_bwd = jnp.flip(rg_lru_scan(jnp.flip(xc, axis=1), wa[1], ba[1], wi[1], bi[1], L[1]), axis=1)
    y = (h_fwd + h_bwd) * jax.nn.gelu(gb.astype(jnp.float32), approximate=True)
    B, S, _ = y.shape
    y = rmsnorm(y.reshape(B, S, LRU_BLOCKS, LRU_BLOCK), norm_g.reshape(LRU_BLOCKS, LRU_BLOCK))
    return y.reshape(B, S, LRU_WIDTH).astype(xb.dtype)


def rotary_tables(S):
    inv_freq = ROPE_THETA ** (-jnp.arange(0, ROT_DIM, 2, dtype=jnp.float32) / ROT_DIM)
    ang = jnp.arange(S, dtype=jnp.float32)[:, None] * inv_freq[None, :]
    return jnp.cos(ang), jnp.sin(ang)


def partial_rope(x, cos, sin):
    half = ROT_DIM // 2
    xr = x[..., :ROT_DIM].astype(jnp.float32)
    x1, x2 = xr[..., :half], xr[..., half:]
    c = cos[None, :, None, None, :]
    s = sin[None, :, None, None, :]
    rot = jnp.concatenate([x1 * c - x2 * s, x2 * c + x1 * s], axis=-1)
    return jnp.concatenate([rot.astype(x.dtype), x[..., ROT_DIM:]], axis=-1)


def differential_attention(q, k, v, lam):
    B, S = q.shape[0], q.shape[1]
    nb = S // Q_BLOCK
    q_blocks = jnp.moveaxis(q.reshape(B, nb, Q_BLOCK, N_HEADS, 2, HEAD_DIM), 1, 0)

    def one_block(qb):
        s = jnp.einsum('bqhcd,bkhcd->bchqk', qb, k).astype(jnp.float32)
        p = jax.nn.softmax(s, axis=-1)
        w = p[:, 0] - lam * p[:, 1]
        return jnp.einsum('bhqk,bkhe->bqhe', w.astype(v.dtype), v)

    o = lax.map(one_block, q_blocks)
    return jnp.moveaxis(o, 0, 1).reshape(B, S, N_HEADS, V_DIM)


def attention_group(q, k, v, cos, sin, lam_params, subln_g, lambda_init):
    B, S, _ = q.shape
    q = partial_rope(q.reshape(B, S, N_HEADS, 2, HEAD_DIM), cos, sin) * (HEAD_DIM ** -0.5)
    k = partial_rope(k.reshape(B, S, N_HEADS, 2, HEAD_DIM), cos, sin)
    v = v.reshape(B, S, N_HEADS, V_DIM)
    lp = lam_params.astype(jnp.float32)
    lam = jnp.exp(jnp.sum(lp[0] * lp[1])) - jnp.exp(jnp.sum(lp[2] * lp[3])) + lambda_init
    o = differential_attention(q, k, v, lam)
    o = rmsnorm(o, subln_g) * (1.0 - lambda_init)
    return o.reshape(B, S, ATT_WIDTH)


def expert_choice_ffn(h, router_w, w_gate, w_up, w_down):
    B, S, D = h.shape
    n = B * S
    cap = max(1, EC_CAPACITY_FACTOR * n // N_EXPERTS)
    t = h.reshape(n, D)
    aff = jax.nn.softmax(jnp.dot(t, router_w).astype(jnp.float32), axis=-1)
    gate, idx = lax.top_k(aff.T, cap)
    xs = t[idx]
    hid = jax.nn.silu(jnp.einsum('ecd,edf->ecf', xs, w_gate)) * jnp.einsum('ecd,edf->ecf', xs, w_up)
    out = jnp.einsum('ecf,efd->ecd', hid, w_down) * gate[..., None].astype(h.dtype)
    y = jnp.zeros((n, D), h.dtype).at[idx.reshape(-1)].add(out.reshape(-1, D))
    return y.reshape(B, S, D)


def encoder_trunk(x, norm1_g, w_in, conv_w, conv_b, lru_wa, lru_ba, lru_wi, lru_bi, lru_L, lru_norm_g,
                  diff_lambda, subln_g, w_out, norm2_g, router_w, w_gate, w_up, w_down, final_g):
    cos, sin = rotary_tables(x.shape[1])
    splits = [LRU_WIDTH, 2 * LRU_WIDTH, 2 * LRU_WIDTH + ATT_WIDTH, 2 * LRU_WIDTH + 2 * ATT_WIDTH]
    for l in range(DEPTH):
        lambda_init = 0.8 - 0.6 * math.exp(-0.3 * l)
        h = rmsnorm(x, norm1_g[l])
        u = jnp.dot(h, w_in[l])
        xb, gb, q, k, v = jnp.split(u, splits, axis=-1)
        y_rec = recurrent_group(xb, gb, conv_w[l], conv_b[l], lru_wa[l], lru_ba[l], lru_wi[l], lru_bi[l],
                                lru_L[l], lru_norm_g[l])
        y_att = attention_group(q, k, v, cos, sin, diff_lambda[l], subln_g[l], lambda_init)
        mixed = jnp.concatenate([y_rec, y_att], axis=-1)
        x = x + jnp.dot(mixed, w_out[l]).astype(x.dtype)
        x = x + expert_choice_ffn(rmsnorm(x, norm2_g[l]), router_w[l], w_gate[l], w_up[l], w_down[l]).astype(x.dtype)
    return rmsnorm(x, final_g)


def setup_inputs(seed: int = 0) -> dict:
    key = jax.random.key(seed)
    ks = jax.random.split(key, 24)

    def nrm(k, shape, scale):
        return jax.random.normal(k, shape, jnp.float32) * scale

    x_prompt = nrm(ks[0], (BATCH, SEQ, D_MODEL), 1.0)
    x_sample = nrm(ks[1], (DEC_BATCH, DEC_SEQ, D_MODEL), 1.0)
    norm1_g = 1.0 + nrm(ks[2], (DEPTH, D_MODEL), 0.01)
    w_in = nrm(ks[3], (DEPTH, D_MODEL, IN_COLS), D_MODEL ** -0.5)
    conv_w = nrm(ks[4], (DEPTH, CONV_WIDTH, LRU_WIDTH), CONV_WIDTH ** -0.5)
    conv_b = nrm(ks[5], (DEPTH, LRU_WIDTH), 0.01)
    lru_wa = nrm(ks[6], (DEPTH, 2, LRU_BLOCKS, LRU_BLOCK, LRU_BLOCK), LRU_BLOCK ** -0.5)
    lru_ba = nrm(ks[7], (DEPTH, 2, LRU_WIDTH), 0.01)
    lru_wi = nrm(ks[8], (DEPTH, 2, LRU_BLOCKS, LRU_BLOCK, LRU_BLOCK), LRU_BLOCK ** -0.5)
    lru_bi = nrm(ks[9], (DEPTH, 2, LRU_WIDTH), 0.01)
    radius = jax.random.uniform(ks[10], (DEPTH, 2, LRU_WIDTH), jnp.float32, LRU_MIN_RAD, LRU_MAX_RAD)
    a_base = radius ** (1.0 / LRU_C)
    lru_L = jnp.log(a_base) - jnp.log1p(-a_base)
    lru_norm_g = 1.0 + nrm(ks[11], (DEPTH, LRU_WIDTH), 0.01)
    diff_lambda = nrm(ks[12], (DEPTH, 4, HEAD_DIM), 0.1)
    subln_g = 1.0 + nrm(ks[13], (DEPTH, V_DIM), 0.01)
    w_out = nrm(ks[14], (DEPTH, D_MODEL, D_MODEL), D_MODEL ** -0.5)
    norm2_g = 1.0 + nrm(ks[15], (DEPTH, D_MODEL), 0.01)
    router_w = nrm(ks[16], (DEPTH, D_MODEL, N_EXPERTS), D_MODEL ** -0.5)
    w_gate = nrm(ks[17], (DEPTH, N_EXPERTS, D_MODEL, EXPERT_FF), D_MODEL ** -0.5)
    w_up = nrm(ks[18], (DEPTH, N_EXPERTS, D_MODEL, EXPERT_FF), D_MODEL ** -0.5)
    w_down = nrm(ks[19], (DEPTH, N_EXPERTS, EXPERT_FF, D_MODEL), EXPERT_FF ** -0.5)
    final_g = 1.0 + nrm(ks[20], (D_MODEL,), 0.01)
    return {'x_prompt': x_prompt, 'x_sample': x_sample, 'norm1_g': norm1_g, 'w_in': w_in,
            'conv_w': conv_w, 'conv_b': conv_b, 'lru_wa': lru_wa, 'lru_ba': lru_ba, 'lru_wi': lru_wi,
            'lru_bi': lru_bi, 'lru_L': lru_L, 'lru_norm_g': lru_norm_g, 'diff_lambda': diff_lambda,
            'subln_g': subln_g, 'w_out': w_out, 'norm2_g': norm2_g, 'router_w': router_w,
            'w_gate': w_gate, 'w_up': w_up, 'w_down': w_down, 'final_g': final_g}


def reference(x_prompt, x_sample, norm1_g, w_in, conv_w, conv_b, lru_wa, lru_ba, lru_wi, lru_bi, lru_L,
              lru_norm_g, diff_lambda, subln_g, w_out, norm2_g, router_w, w_gate, w_up, w_down, final_g):
    y_prompt = encoder_trunk(x_prompt, norm1_g, w_in, conv_w, conv_b, lru_wa, lru_ba, lru_wi, lru_bi, lru_L,
                             lru_norm_g, diff_lambda, subln_g, w_out, norm2_g, router_w, w_gate, w_up, w_down,
                             final_g)
    y_sample = encoder_trunk(x_sample, norm1_g, w_in, conv_w, conv_b, lru_wa, lru_ba, lru_wi, lru_bi, lru_L,
                             lru_norm_g, diff_lambda, subln_g, w_out, norm2_g, router_w, w_gate, w_up, w_down,
                             final_g)
    return (y_prompt, y_sample)
```

```python
import functools
import math

import jax
import jax.numpy as jnp
from jax import lax
from jax.experimental import pallas as pl
from jax.experimental.pallas import tpu as pltpu

_LANES = 128
_SUBLANES = 8
_HEAD_DIM = 128
_V_DIM = 2 * _HEAD_DIM
_LRU_BLOCK = 256
_ROT_DIM = _HEAD_DIM // 4
_CONV_WIDTH = 4
_CONV_LEFT = 2
_LRU_C = 8.0
_ROPE_THETA = 500000.0
_NORM_EPS = 1e-6
_EC_CAPACITY_FACTOR = 2
_LAMBDA_INIT = 0.8 - 0.6 * math.exp(-0.3 * 0)
_V7X_VMEM_BYTES = 64 * 2**20

_F32 = jnp.float32
_BF16 = jnp.bfloat16


def _vmem_limit(estimate_bytes):
    return int(min(_V7X_VMEM_BYTES - 6 * 2**20, max(32 * 2**20, estimate_bytes + 12 * 2**20)))


def _params(semantics, estimate_bytes):
    return pltpu.CompilerParams(dimension_semantics=semantics, vmem_limit_bytes=_vmem_limit(estimate_bytes))


def _rms_scale(x, g):
    ms = jnp.mean(x * x, axis=-1, keepdims=True)
    return x * lax.rsqrt(ms + _NORM_EPS) * g


def _in_proj_kernel(x_ref, g_ref, w_ref, c_ref, sa_ref, sb_ref, o_ref, hn_ref, *, q_lo, k_lo, v_lo):
    j = pl.program_id(1)

    @pl.when(j == 0)
    def _():
        hn_ref[...] = _rms_scale(x_ref[...], g_ref[...]).astype(_BF16)

    acc = jnp.dot(hn_ref[...], w_ref[...], preferred_element_type=_F32)
    is_rope = jnp.logical_and(j >= q_lo, j < v_lo)

    @pl.when(is_rope)
    def _():
        scale = jnp.where(j < k_lo, _HEAD_DIM ** -0.5, 1.0).astype(_F32)
        c, sa, sb = c_ref[...], sa_ref[...], sb_ref[...]
        for grp in range(acc.shape[1] // _LANES):
            blk = acc[:, grp * _LANES:(grp + 1) * _LANES]
            rot = (blk * c
                   + pltpu.roll(blk, _LANES - _ROT_DIM // 2, 1) * sa
                   + pltpu.roll(blk, _ROT_DIM // 2, 1) * sb)
            o_ref[:, grp * _LANES:(grp + 1) * _LANES] = (rot * scale).astype(o_ref.dtype)

    @pl.when(jnp.logical_not(is_rope))
    def _():
        o_ref[...] = acc.astype(o_ref.dtype)


def _in_proj(x2d, g1, w_in, rope_c, rope_sa, rope_sb, *, seq, lru_w, att_w):
    n, d = x2d.shape
    cols = w_in.shape[1]
    tm = min(512, seq)
    tn = min(1024, lru_w, att_w)
    q_lo = 2 * lru_w // tn
    k_lo = q_lo + att_w // tn
    v_lo = k_lo + att_w // tn
    sblocks = seq // tm
    est = 2 * tm * d * 4 + tm * d * 2 + 2 * d * tn * 2 + 2 * tm * tn * 2 + 6 * tm * _LANES * 4 + 2 * tm * tn * 4
    return pl.pallas_call(
        functools.partial(_in_proj_kernel, q_lo=q_lo, k_lo=k_lo, v_lo=v_lo),
        grid=(n // tm, cols // tn),
        in_specs=[
            pl.BlockSpec((tm, d), lambda i, j: (i, 0)),
            pl.BlockSpec((1, d), lambda i, j: (0, 0)),
            pl.BlockSpec((d, tn), lambda i, j: (0, j)),
            pl.BlockSpec((tm, _LANES), lambda i, j: (i % sblocks, 0)),
            pl.BlockSpec((tm, _LANES), lambda i, j: (i % sblocks, 0)),
            pl.BlockSpec((tm, _LANES), lambda i, j: (i % sblocks, 0)),
        ],
        out_specs=pl.BlockSpec((tm, tn), lambda i, j: (i, j)),
        out_shape=jax.ShapeDtypeStruct((n, cols), _BF16),
        scratch_shapes=[pltpu.VMEM((tm, d), _BF16)],
        compiler_params=_params(("arbitrary", "arbitrary"), est),
        name="in_proj",
    )(x2d, g1, w_in, rope_c, rope_sa, rope_sb)


def _softplus(z):
    return jnp.maximum(z, 0.0) + jnp.log1p(jnp.exp(-jnp.abs(z)))


def _rglru_kernel(xb_ref, gb_ref, cw_ref, cb_ref, wa_ref, ba_ref, wi_ref, bi_ref, l_ref, ng_ref, o_ref,
                  xf_ref, af_ref, bf_ref, ab_ref, bb_ref, *, seq, chunk):
    w = xb_ref.shape[1]
    pad = _SUBLANES
    nchunks = seq // chunk

    xf_ref[0:pad, :] = jnp.zeros((pad, w), _F32)
    xf_ref[seq + pad:seq + 2 * pad, :] = jnp.zeros((pad, w), _F32)

    def fill(c, carry):
        t0 = pl.multiple_of(c * chunk, chunk)
        xf_ref[pl.ds(t0 + pad, chunk), :] = xb_ref[pl.ds(t0, chunk), :].astype(_F32)
        return carry

    lax.fori_loop(0, nchunks, fill, 0)

    cw = cw_ref[...]
    cb = cb_ref[...]
    sp = _softplus(-l_ref[...])
    a_refs = (af_ref, ab_ref)
    b_refs = (bf_ref, bb_ref)

    def gates(c, carry):
        t0 = pl.multiple_of(c * chunk, chunk)
        xwin = xf_ref[pl.ds(t0, chunk + 2 * pad), :]
        xc = cb
        for k in range(_CONV_WIDTH):
            lo = pad - _CONV_LEFT + k
            xc = xc + xwin[lo:lo + chunk, :] * cw[k:k + 1, :]
        xcb = xc.astype(_BF16)
        for d in range(2):
            r = jax.nn.sigmoid(jnp.dot(xcb, wa_ref[d], preferred_element_type=_F32) + ba_ref[d:d + 1, :])
            i = jax.nn.sigmoid(jnp.dot(xcb, wi_ref[d], preferred_element_type=_F32) + bi_ref[d:d + 1, :])
            log_a = (-_LRU_C) * r * sp[d:d + 1, :]
            a = jnp.exp(log_a)
            a_refs[d][pl.ds(t0, chunk), :] = a
            b_refs[d][pl.ds(t0, chunk), :] = jnp.sqrt(jnp.tanh(-log_a) * (1.0 + a * a)) * (i * xc)
        return carry

    lax.fori_loop(0, nchunks, gates, 0)

    row = lax.broadcasted_iota(jnp.int32, (_SUBLANES, w), 0)
    ntiles = seq // _SUBLANES

    def scan(j, carry):
        cf, cbk = carry
        tf = pl.multiple_of(j * _SUBLANES, _SUBLANES)
        tb = pl.multiple_of((ntiles - 1 - j) * _SUBLANES, _SUBLANES)
        a = af_ref[pl.ds(tf, _SUBLANES), :]
        b = bf_ref[pl.ds(tf, _SUBLANES), :]
        a2 = ab_ref[pl.ds(tb, _SUBLANES), :]
        b2 = bb_ref[pl.ds(tb, _SUBLANES), :]
        for s in (1, 2, 4):
            keep = row >= s
            a_s = jnp.where(keep, pltpu.roll(a, s, 0), 1.0)
            b_s = jnp.where(keep, pltpu.roll(b, s, 0), 0.0)
            b = a * b_s + b
            a = a * a_s
            keep2 = row < _SUBLANES - s
            a2_s = jnp.where(keep2, pltpu.roll(a2, _SUBLANES - s, 0), 1.0)
            b2_s = jnp.where(keep2, pltpu.roll(b2, _SUBLANES - s, 0), 0.0)
            b2 = a2 * b2_s + b2
            a2 = a2 * a2_s
        h = a * cf + b
        h2 = a2 * cbk + b2
        bf_ref[pl.ds(tf, _SUBLANES), :] = h
        bb_ref[pl.ds(tb, _SUBLANES), :] = h2
        cf = jnp.broadcast_to(h[_SUBLANES - 1:_SUBLANES, :], (_SUBLANES, w))
        cbk = jnp.broadcast_to(h2[0:1, :], (_SUBLANES, w))
        return cf, cbk

    zero = jnp.zeros((_SUBLANES, w), _F32)
    lax.fori_loop(0, ntiles, scan, (zero, zero), unroll=4)

    ng = ng_ref[...]

    def finish(c, carry):
        t0 = pl.multiple_of(c * chunk, chunk)
        gate = jax.nn.gelu(gb_ref[pl.ds(t0, chunk), :].astype(_F32), approximate=True)
        y = (bf_ref[pl.ds(t0, chunk), :] + bb_ref[pl.ds(t0, chunk), :]) * gate
        o_ref[pl.ds(t0, chunk), :] = _rms_scale(y, ng).astype(o_ref.dtype)
        return carry

    lax.fori_loop(0, nchunks, finish, 0)


def _rglru(u, conv_w, conv_b, wa, ba, wi, bi, lam, norm_g, *, batch, seq, lru_w):
    nblk = lru_w // _LRU_BLOCK
    w = _LRU_BLOCK
    chunk = min(256, seq)
    est = 6 * seq * w * 2 + (seq + 16) * w * 4 + 4 * seq * w * 4 + 8 * w * w * 2 + 16 * chunk * w * 4
    blk = lambda shape, imap: pl.BlockSpec(shape, imap)
    return pl.pallas_call(
        functools.partial(_rglru_kernel, seq=seq, chunk=chunk),
        grid=(batch, nblk),
        in_specs=[
            blk((seq, w), lambda b, n: (b, n)),
            blk((seq, w), lambda b, n: (b, nblk + n)),
            blk((_CONV_WIDTH, w), lambda b, n: (0, n)),
            blk((1, w), lambda b, n: (0, n)),
            blk((2, None, w, w), lambda b, n: (0, n, 0, 0)),
            blk((2, w), lambda b, n: (0, n)),
            blk((2, None, w, w), lambda b, n: (0, n, 0, 0)),
            blk((2, w), lambda b, n: (0, n)),
            blk((2, w), lambda b, n: (0, n)),
            blk((1, w), lambda b, n: (0, n)),
        ],
        out_specs=blk((seq, w), lambda b, n: (b, n)),
        out_shape=jax.ShapeDtypeStruct((batch * seq, lru_w), _BF16),
        scratch_shapes=[pltpu.VMEM((seq + 2 * _SUBLANES, w), _F32)] + [pltpu.VMEM((seq, w), _F32)] * 4,
        compiler_params=_params(("arbitrary", "arbitrary"), est),
        name="rglru",
    )(u, u, conv_w, conv_b, wa, ba, wi, bi, lam, norm_g)


def _diff_attn_kernel(q_ref, k_ref, v_ref, lp_ref, g_ref, o_ref):
    lp = lp_ref[...]
    lam = (jnp.exp(jnp.sum(lp[0:1] * lp[1:2], axis=-1, keepdims=True))
           - jnp.exp(jnp.sum(lp[2:3] * lp[3:4], axis=-1, keepdims=True)) + _LAMBDA_INIT)
    nt = (((1,), (1,)), ((), ()))
    parts = []
    for c in range(2):
        qc = q_ref[:, c * _HEAD_DIM:(c + 1) * _HEAD_DIM]
        kc = k_ref[:, c * _HEAD_DIM:(c + 1) * _HEAD_DIM]
        s = lax.dot_general(qc, kc, nt, preferred_element_type=_F32)
        e = jnp.exp(s - jnp.max(s, axis=-1, keepdims=True))
        parts.append((e, 1.0 / jnp.sum(e, axis=-1, keepdims=True)))
    (e0, r0), (e1, r1) = parts
    wgt = (e0 * r0 - e1 * (lam * r1)).astype(_BF16)
    o = jnp.dot(wgt, v_ref[...], preferred_element_type=_F32)
    o_ref[...] = (_rms_scale(o, g_ref[...]) * (1.0 - _LAMBDA_INIT)).astype(o_ref.dtype)


def _diff_attn(u, lam_params, subln_g, *, batch, seq, lru_w, att_w):
    heads = att_w // _V_DIM
    tq = min(256, seq)
    qblocks = seq // tq
    q0 = 2 * lru_w // _V_DIM
    k0 = q0 + heads
    v0 = k0 + heads
    est = 2 * tq * _V_DIM * 2 + 4 * seq * _V_DIM * 2 + 2 * tq * _V_DIM * 2 + 6 * tq * seq * 4
    return pl.pallas_call(
        _diff_attn_kernel,
        grid=(batch, heads, qblocks),
        in_specs=[
            pl.BlockSpec((tq, _V_DIM), lambda b, h, i: (b * qblocks + i, q0 + h)),
            pl.BlockSpec((seq, _V_DIM), lambda b, h, i: (b, k0 + h)),
            pl.BlockSpec((seq, _V_DIM), lambda b, h, i: (b, v0 + h)),
            pl.BlockSpec((4, _HEAD_DIM), lambda b, h, i: (0, 0)),
            pl.BlockSpec((1, _V_DIM), lambda b, h, i: (0, 0)),
        ],
        out_specs=pl.BlockSpec((tq, _V_DIM), lambda b, h, i: (b * qblocks + i, h)),
        out_shape=jax.ShapeDtypeStruct((batch * seq, att_w), _BF16),
        compiler_params=_params(("arbitrary", "arbitrary", "arbitrary"), est),
        name="diff_attn",
    )(u, u, u, lam_params, subln_g)


def _out_proj_kernel(x_ref, a_ref, b_ref, wa_ref, wb_ref, o_ref):
    acc = jnp.dot(a_ref[...], wa_ref[...], preferred_element_type=_F32)
    acc = acc + jnp.dot(b_ref[...], wb_ref[...], preferred_element_type=_F32)
    o_ref[...] = x_ref[...] + acc


def _out_proj(x2d, y_rec, y_att, w_out):
    n, d = x2d.shape
    ka, kb = y_rec.shape[1], y_att.shape[1]
    tm = min(512, n)
    tn = min(1024, d)
    kblocks_a = ka // kb
    est = 2 * tm * tn * 4 * 2 + 2 * tm * (ka + kb) * 2 + 2 * (ka + kb) * tn * 2 + tm * tn * 4
    return pl.pallas_call(
        _out_proj_kernel,
        grid=(n // tm, d // tn),
        in_specs=[
            pl.BlockSpec((tm, tn), lambda i, j: (i, j)),
            pl.BlockSpec((tm, ka), lambda i, j: (i, 0)),
            pl.BlockSpec((tm, kb), lambda i, j: (i, 0)),
            pl.BlockSpec((ka, tn), lambda i, j: (0, j)),
            pl.BlockSpec((kb, tn), lambda i, j: (kblocks_a, j)),
        ],
        out_specs=pl.BlockSpec((tm, tn), lambda i, j: (i, j)),
        out_shape=jax.ShapeDtypeStruct((n, d), _F32),
        compiler_params=_params(("arbitrary", "arbitrary"), est),
        name="out_proj",
    )(x2d, y_rec, y_att, w_out, w_out)


def _router_kernel(x_ref, g_ref, rw_ref, o_ref):
    hn = _rms_scale(x_ref[...], g_ref[...]).astype(_BF16)
    o_ref[...] = lax.dot_general(rw_ref[...], hn, (((1,), (1,)), ((), ())), preferred_element_type=_F32)


def _router_logits(x2d, g2, rw_t):
    n, d = x2d.shape
    e = rw_t.shape[0]
    tm = min(512, n)
    est = 2 * tm * d * 4 + tm * d * 2 + 2 * e * d * 2 + 2 * e * tm * 4 + tm * d * 4
    return pl.pallas_call(
        _router_kernel,
        grid=(n // tm,),
        in_specs=[
            pl.BlockSpec((tm, d), lambda i: (i, 0)),
            pl.BlockSpec((1, d), lambda i: (0, 0)),
            pl.BlockSpec((e, d), lambda i: (0, 0)),
        ],
        out_specs=pl.BlockSpec((e, tm), lambda i: (0, i)),
        out_shape=jax.ShapeDtypeStruct((e, n), _F32),
        compiler_params=_params(("arbitrary",), est),
        name="router",
    )(x2d, g2, rw_t)


def _count(mask_f32):
    return jnp.sum(jnp.sum(mask_f32, axis=0, keepdims=True), axis=1, keepdims=True)


def _route_kernel(lg_ref, lgt_ref, idx_ref, gate_ref, aff_ref, afft_ref, *, nblocks, cap):
    n_exp, nbp, _ = lg_ref.shape

    def softmax0(l):
        ex = jnp.exp(l - jnp.max(l, axis=0, keepdims=True))
        return ex / jnp.sum(ex, axis=0, keepdims=True)

    aff_ref[...] = softmax0(lg_ref[...])
    afft_ref[...] = softmax0(lgt_ref[...])

    f32 = lambda m: jnp.where(m, 1.0, 0.0).astype(_F32)
    bf = lambda x: x.astype(_BF16)
    mm = lambda a, b: jnp.dot(a, b, preferred_element_type=_F32)

    ii = lambda shape, dim: lax.broadcasted_iota(jnp.int32, shape, dim)
    u_incl = bf(f32(ii((_LANES, _LANES), 0) <= ii((_LANES, _LANES), 1)))
    l_incl = bf(f32(ii((_LANES, _LANES), 1) <= ii((_LANES, _LANES), 0)))
    l_strict = bf(f32(ii((nbp, nbp), 1) < ii((nbp, nbp), 0)))
    u_strict = bf(f32(ii((nbp, nbp), 0) < ii((nbp, nbp), 1)))
    ones_ll = jnp.ones((_LANES, _LANES), _BF16)
    ones_8l = jnp.ones((_SUBLANES, _LANES), _BF16)
    ones_8b = jnp.ones((_SUBLANES, nbp), _BF16)
    blk_a = ii((nbp, _LANES), 0)
    blk_t = ii((_LANES, nbp), 1)
    p_row = ii((1, cap), 1).astype(_F32)
    blk_p = ii((nbp, cap), 0).astype(_F32)
    lane_p = ii((_LANES, cap), 0).astype(_F32)

    def per_expert(e, carry):
        a = aff_ref[e]
        at = afft_ref[e]
        bits = jnp.where(blk_a < nblocks, pltpu.bitcast(a, jnp.int32), -1)
        bits_t = jnp.where(blk_t < nblocks, pltpu.bitcast(at, jnp.int32), -1)

        def bit_step(i, thr):
            cand = thr | lax.shift_left(jnp.int32(1), 30 - i)
            return jnp.where(_count(f32(bits >= cand)) >= cap, cand, thr)

        thr = lax.fori_loop(0, 31, bit_step, jnp.zeros((1, 1), jnp.int32))
        gt, tie = f32(bits > thr), f32(bits == thr)
        gt_t, tie_t = f32(bits_t > thr), f32(bits_t == thr)
        need = cap - _count(gt)

        tie_b = bf(tie)
        rank = mm(l_strict, bf(mm(tie_b, ones_ll))) + mm(tie_b, u_incl) - tie
        sel = bf(jnp.where(rank < need, tie, 0.0) + gt)
        tie_tb = bf(tie_t)
        rank_t = mm(bf(mm(ones_8l, tie_tb)), u_strict)[0:1, :] + mm(l_incl, tie_tb) - tie_t
        sel_t = bf(jnp.where(rank_t < need, tie_t, 0.0) + gt_t)

        tot_b = mm(sel, ones_ll)
        cend = mm(l_strict, bf(tot_b)) + tot_b
        cl_t = bf(mm(l_incl, sel_t))
        tot_row = bf(mm(ones_8l, sel_t))

        ind_le = bf(f32(cend[:, 0:1] <= p_row))
        blk_of_p = mm(ones_8b, ind_le)[0:1, :]
        p_local = p_row - mm(tot_row, ind_le)[0:1, :]
        onehot = bf(f32(blk_p == blk_of_p))
        cl_of_p = mm(cl_t, onehot)
        off = mm(ones_8l, bf(f32(cl_of_p <= p_local)))[0:1, :]
        idx_ref[e] = (blk_of_p * _LANES + off).astype(jnp.int32)

        hi = bf(at)
        r1 = at - hi.astype(_F32)
        mid = bf(r1)
        lo = bf(r1 - mid.astype(_F32))
        g_of_p = mm(hi, onehot) + mm(mid, onehot) + mm(lo, onehot)
        gate_ref[e] = jnp.sum(jnp.where(lane_p == off, g_of_p, 0.0), axis=0, keepdims=True)
        return carry

    lax.fori_loop(0, n_exp, per_expert, 0)


def _route(logits_t, *, cap):
    e, n = logits_t.shape
    nblocks = n // _LANES
    nbp = max(nblocks, _LANES)
    lg = logits_t.reshape(e, nblocks, _LANES)
    lg = jnp.pad(lg, ((0, 0), (0, nbp - nblocks), (0, 0)))
    lgt = jnp.swapaxes(lg, 1, 2)
    est = 6 * e * nbp * _LANES * 4 + 12 * max(nbp, _LANES) * cap * 4
    full = lambda shape: pl.BlockSpec(shape, lambda: (0,) * len(shape))
    idx, gate = pl.pallas_call(
        functools.partial(_route_kernel, nblocks=nblocks, cap=cap),
        in_specs=[full((e, nbp, _LANES)), full((e, _LANES, nbp))],
        out_specs=[full((e, 1, cap)), full((e, 1, cap))],
        out_shape=[jax.ShapeDtypeStruct((e, 1, cap), jnp.int32), jax.ShapeDtypeStruct((e, 1, cap), _F32)],
        scratch_shapes=[pltpu.VMEM((e, nbp, _LANES), _F32), pltpu.VMEM((e, _LANES, nbp), _F32)],
        compiler_params=pltpu.CompilerParams(vmem_limit_bytes=_vmem_limit(est)),
        name="route",
    )(lg, lgt)
    return idx.reshape(e * cap), gate.reshape(e * cap, 1)


def _row_copy(src, src_row, dst, dst_row, sem):
    return pltpu.make_async_copy(src.at[pl.ds(src_row, 1)], dst.at[pl.ds(dst_row, 1)], sem)


def _gather_kernel(idx_ref, x_hbm, o_hbm, sem, *, rows):
    base = pl.program_id(0) * rows

    def start(r, carry):
        _row_copy(x_hbm, idx_ref[base + r], o_hbm, base + r, sem).start()
        return carry

    def wait(r, carry):
        _row_copy(x_hbm, idx_ref[base + r], o_hbm, base + r, sem).wait()
        return carry

    lax.fori_loop(0, rows, start, 0)
    lax.fori_loop(0, rows, wait, 0)


def _gather_rows(idx, x2d):
    slots = idx.shape[0]
    d = x2d.shape[1]
    rows = min(512, slots)
    return pl.pallas_call(
        functools.partial(_gather_kernel, rows=rows),
        grid_spec=pltpu.PrefetchScalarGridSpec(
            num_scalar_prefetch=1,
            grid=(slots // rows,),
            in_specs=[pl.BlockSpec(memory_space=pl.ANY)],
            out_specs=pl.BlockSpec(memory_space=pl.ANY),
            scratch_shapes=[pltpu.SemaphoreType.DMA(())],
        ),
        out_shape=jax.ShapeDtypeStruct((slots, d), x2d.dtype),
        compiler_params=pltpu.CompilerParams(dimension_semantics=("arbitrary",)),
        name="gather",
    )(idx, x2d)


def _ffn_up_kernel(x_ref, g_ref, wg_ref, wu_ref, o_ref, hn_ref):
    @pl.when(pl.program_id(1) == 0)
    def _():
        hn_ref[...] = _rms_scale(x_ref[...], g_ref[...]).astype(_BF16)

    hn = hn_ref[...]
    gate = jnp.dot(hn, wg_ref[...], preferred_element_type=_F32)
    up = jnp.dot(hn, wu_ref[...], preferred_element_type=_F32)
    o_ref[...] = (jax.nn.silu(gate) * up).astype(o_ref.dtype)


def _ffn_up(xs, g2, w_gate, w_up, *, cap):
    slots, d = xs.shape
    ff = w_gate.shape[2]
    tm = min(512, cap)
    tf = min(512, ff)
    per_e = cap // tm
    est = 2 * tm * d * 4 + tm * d * 2 + 4 * d * tf * 2 + 2 * tm * tf * 2 + 3 * tm * tf * 4 + tm * d * 4
    return pl.pallas_call(
        _ffn_up_kernel,
        grid=(slots // tm, ff // tf),
        in_specs=[
            pl.BlockSpec((tm, d), lambda g, f: (g, 0)),
            pl.BlockSpec((1, d), lambda g, f: (0, 0)),
            pl.BlockSpec((None, d, tf), lambda g, f: (g // per_e, 0, f)),
            pl.BlockSpec((None, d, tf), lambda g, f: (g // per_e, 0, f)),
        ],
        out_specs=pl.BlockSpec((tm, tf), lambda g, f: (g, f)),
        out_shape=jax.ShapeDtypeStruct((slots, ff), _BF16),
        scratch_shapes=[pltpu.VMEM((tm, d), _BF16)],
        compiler_params=_params(("arbitrary", "arbitrary"), est),
        name="ffn_up",
    )(xs, g2, w_gate, w_up)


def _ffn_down_kernel(h_ref, w_ref, gate_ref, o_ref):
    o_ref[...] = jnp.dot(h_ref[...], w_ref[...], preferred_element_type=_F32) * gate_ref[...]


def _ffn_down(hid, w_down, gates, *, cap):
    slots, ff = hid.shape
    d = w_down.shape[2]
    tm = min(512, cap)
    tn = min(1024, d)
    per_e = cap // tm
    est = 2 * tm * ff * 2 + 2 * ff * tn * 2 + 2 * tm * tn * 4 + 2 * tm * _LANES * 4 + tm * tn * 4
    return pl.pallas_call(
        _ffn_down_kernel,
        grid=(slots // tm, d // tn),
        in_specs=[
            pl.BlockSpec((tm, ff), lambda g, j: (g, 0)),
            pl.BlockSpec((None, ff, tn), lambda g, j: (g // per_e, 0, j)),
            pl.BlockSpec((tm, 1), lambda g, j: (g, 0)),
        ],
        out_specs=pl.BlockSpec((tm, tn), lambda g, j: (g, j)),
        out_shape=jax.ShapeDtypeStruct((slots, d), _F32),
        compiler_params=_params(("arbitrary", "arbitrary"), est),
        name="ffn_down",
    )(hid, w_down, gates)


def _combine_kernel(idx_ref, x_hbm, y_ref, o_hbm, buf, sem_in, sem_out, *, rows):
    del x_hbm
    base = pl.program_id(0) * rows

    def fetch(r, carry):
        _row_copy(o_hbm, idx_ref[base + r], buf, r, sem_in).start()
        return carry

    def fetched(r, carry):
        _row_copy(o_hbm, idx_ref[base + r], buf, r, sem_in).wait()
        return carry

    def put(r, carry):
        _row_copy(buf, r, o_hbm, idx_ref[base + r], sem_out).start()
        return carry

    def put_done(r, carry):
        _row_copy(buf, r, o_hbm, idx_ref[base + r], sem_out).wait()
        return carry

    lax.fori_loop(0, rows, fetch, 0)
    lax.fori_loop(0, rows, fetched, 0)
    buf[...] = buf[...] + y_ref[...]
    lax.fori_loop(0, rows, put, 0)
    lax.fori_loop(0, rows, put_done, 0)


def _combine(idx, x2d, outs, *, cap):
    slots, d = outs.shape
    rows = min(256, cap)
    assert cap % rows == 0
    est = 3 * rows * d * 4 + 2 * rows * d * 4
    return pl.pallas_call(
        functools.partial(_combine_kernel, rows=rows),
        grid_spec=pltpu.PrefetchScalarGridSpec(
            num_scalar_prefetch=1,
            grid=(slots // rows,),
            in_specs=[pl.BlockSpec(memory_space=pl.ANY),
                      pl.BlockSpec((rows, d), lambda g, idx_ref: (g, 0))],
            out_specs=pl.BlockSpec(memory_space=pl.ANY),
            scratch_shapes=[pltpu.VMEM((rows, d), _F32), pltpu.SemaphoreType.DMA(()), pltpu.SemaphoreType.DMA(())],
        ),
        out_shape=jax.ShapeDtypeStruct(x2d.shape, x2d.dtype),
        input_output_aliases={1: 0},
        compiler_params=pltpu.CompilerParams(dimension_semantics=("arbitrary",), vmem_limit_bytes=_vmem_limit(est),
                                             has_side_effects=True),
        name="combine",
    )(idx, x2d, outs)


def _final_norm_kernel(x_ref, g_ref, o_ref):
    o_ref[...] = _rms_scale(x_ref[...], g_ref[...])


def _final_norm(x2d, g):
    n, d = x2d.shape
    tm = min(512, n)
    return pl.pallas_call(
        _final_norm_kernel,
        grid=(n // tm,),
        in_specs=[pl.BlockSpec((tm, d), lambda i: (i, 0)), pl.BlockSpec((1, d), lambda i: (0, 0))],
        out_specs=pl.BlockSpec((tm, d), lambda i: (i, 0)),
        out_shape=jax.ShapeDtypeStruct((n, d), _F32),
        compiler_params=_params(("arbitrary",), 5 * tm * d * 4),
        name="final_norm",
    )(x2d, g)


def _rope_tables(seq):
    half = _ROT_DIM // 2
    inv_freq = _ROPE_THETA ** (-jnp.arange(0, _ROT_DIM, 2, dtype=_F32) / _ROT_DIM)
    ang = jnp.arange(seq, dtype=_F32)[:, None] * inv_freq[None, :]
    cos, sin = jnp.cos(ang), jnp.sin(ang)
    zeros = lambda k: jnp.zeros((seq, k), _F32)
    c = jnp.concatenate([cos, cos, jnp.ones((seq, _LANES - _ROT_DIM), _F32)], axis=1)
    sa = jnp.concatenate([-sin, zeros(_LANES - half)], axis=1)
    sb = jnp.concatenate([zeros(half), sin, zeros(_LANES - _ROT_DIM)], axis=1)
    return c, sa, sb


def _trunk(x, p):
    batch, seq, d = x.shape
    n = batch * seq
    lru_w = p["conv_w"].shape[1]
    att_w = (p["w_in"].shape[1] - 2 * lru_w) // 3
    n_exp = p["rw_t"].shape[0]
    cap = max(1, _EC_CAPACITY_FACTOR * n // n_exp)
    x2d = x.reshape(n, d)

    u = _in_proj(x2d, p["norm1_g"], p["w_in"], *p["rope"], seq=seq, lru_w=lru_w, att_w=att_w)
    y_rec = _rglru(u, p["conv_w"], p["conv_b"], p["lru_wa"], p["lru_ba"], p["lru_wi"], p["lru_bi"], p["lru_L"],
                   p["lru_norm_g"], batch=batch, seq=seq, lru_w=lru_w)
    y_att = _diff_attn(u, p["diff_lambda"], p["subln_g"], batch=batch, seq=seq, lru_w=lru_w, att_w=att_w)
    x1 = _out_proj(x2d, y_rec, y_att, p["w_out"])

    logits_t = _router_logits(x1, p["norm2_g"], p["rw_t"])
    idx, gates = _route(logits_t, cap=cap)
    xs = _gather_rows(idx, x1)
    hid = _ffn_up(xs, p["norm2_g"], p["w_gate"], p["w_up"], cap=cap)
    outs = _ffn_down(hid, p["w_down"], gates, cap=cap)
    x2 = _combine(idx, x1, outs, cap=cap)
    return _final_norm(x2, p["final_g"]).reshape(batch, seq, d)


def kernel(x_prompt, x_sample, norm1_g, w_in, conv_w, conv_b, lru_wa, lru_ba, lru_wi, lru_bi, lru_L, lru_norm_g,
           diff_lambda, subln_g, w_out, norm2_g, router_w, w_gate, w_up, w_down, final_g):
    assert norm1_g.shape[0] == 1, "single-layer trunk"
    row = lambda v: v.reshape(1, -1).astype(_F32)
    p = {
        "norm1_g": row(norm1_g[0]), "w_in": w_in[0].astype(_BF16),
        "conv_w": conv_w[0], "conv_b": row(conv_b[0]),
        "lru_wa": lru_wa[0].astype(_BF16), "lru_ba": lru_ba[0], "lru_wi": lru_wi[0].astype(_BF16),
        "lru_bi": lru_bi[0], "lru_L": lru_L[0], "lru_norm_g": row(lru_norm_g[0]),
        "diff_lambda": diff_lambda[0], "subln_g": row(subln_g[0]),
        "w_out": w_out[0].astype(_BF16), "norm2_g": row(norm2_g[0]),
        "rw_t": router_w[0].T.astype(_BF16),
        "w_gate": w_gate[0].astype(_BF16), "w_up": w_up[0].astype(_BF16), "w_down": w_down[0].astype(_BF16),
        "final_g": row(final_g),
        "rope": _rope_tables(x_prompt.shape[1]),
    }
    assert x_prompt.shape[1] == x_sample.shape[1], "both groups share the rotary tables"
    return _trunk(x_prompt, p), _trunk(x_sample, p)
```

```python
import functools
import math

import jax
import jax.numpy as jnp
from jax import lax
from jax.experimental import pallas as pl
from jax.experimental.pallas import tpu as pltpu

_LANES = 128
_SUBLANES = 8
_HEAD_DIM = 128
_V_DIM = 2 * _HEAD_DIM
_LRU_BLOCK = 256
_ROT_DIM = _HEAD_DIM // 4
_CONV_WIDTH = 4
_CONV_LEFT = 2
_LRU_C = 8.0
_ROPE_THETA = 500000.0
_NORM_EPS = 1e-6
_EC_CAPACITY_FACTOR = 2
_LAMBDA_INIT = 0.8 - 0.6 * math.exp(-0.3 * 0)
_V7X_VMEM_BYTES = 64 * 2**20

_F32 = jnp.float32
_BF16 = jnp.bfloat16


def _vmem_limit(estimate_bytes):
    return int(min(_V7X_VMEM_BYTES - 6 * 2**20, max(32 * 2**20, estimate_bytes + 12 * 2**20)))


def _params(semantics, estimate_bytes):
    return pltpu.CompilerParams(dimension_semantics=semantics, vmem_limit_bytes=_vmem_limit(estimate_bytes))


def _rms_scale(x, g):
    ms = jnp.mean(x * x, axis=-1, keepdims=True)
    return x * lax.rsqrt(ms + _NORM_EPS) * g


def _in_proj_kernel(x_ref, g_ref, w_ref, c_ref, sa_ref, sb_ref, o_ref, hn_ref, *, q_lo, k_lo, v_lo):
    j = pl.program_id(1)

    @pl.when(j == 0)
    def _():
        hn_ref[...] = _rms_scale(x_ref[...], g_ref[...]).astype(_BF16)

    acc = jnp.dot(hn_ref[...], w_ref[...], preferred_element_type=_F32)
    is_rope = jnp.logical_and(j >= q_lo, j < v_lo)

    @pl.when(is_rope)
    def _():
        scale = jnp.where(j < k_lo, _HEAD_DIM ** -0.5 * math.log2(math.e), 1.0).astype(_F32)
        c, sa, sb = c_ref[...], sa_ref[...], sb_ref[...]
        for grp in range(acc.shape[1] // _LANES):
            blk = acc[:, grp * _LANES:(grp + 1) * _LANES]
            rot = (blk * c
                   + pltpu.roll(blk, _LANES - _ROT_DIM // 2, 1) * sa
                   + pltpu.roll(blk, _ROT_DIM // 2, 1) * sb)
            o_ref[:, grp * _LANES:(grp + 1) * _LANES] = (rot * scale).astype(o_ref.dtype)

    @pl.when(jnp.logical_not(is_rope))
    def _():
        o_ref[...] = acc.astype(o_ref.dtype)


def _in_proj(x2d, g1, w_in, rope_c, rope_sa, rope_sb, *, seq, lru_w, att_w):
    n, d = x2d.shape
    cols = w_in.shape[1]
    tm = min(512, seq)
    tn = min(1024, lru_w, att_w)
    q_lo = 2 * lru_w // tn
    k_lo = q_lo + att_w // tn
    v_lo = k_lo + att_w // tn
    sblocks = seq // tm
    est = 2 * tm * d * 4 + tm * d * 2 + 2 * d * tn * 2 + 2 * tm * tn * 2 + 6 * tm * _LANES * 4 + 2 * tm * tn * 4
    return pl.pallas_call(
        functools.partial(_in_proj_kernel, q_lo=q_lo, k_lo=k_lo, v_lo=v_lo),
        grid=(n // tm, cols // tn),
        in_specs=[
            pl.BlockSpec((tm, d), lambda i, j: (i, 0)),
            pl.BlockSpec((1, d), lambda i, j: (0, 0)),
            pl.BlockSpec((d, tn), lambda i, j: (0, j)),
            pl.BlockSpec((tm, _LANES), lambda i, j: (i % sblocks, 0)),
            pl.BlockSpec((tm, _LANES), lambda i, j: (i % sblocks, 0)),
            pl.BlockSpec((tm, _LANES), lambda i, j: (i % sblocks, 0)),
        ],
        out_specs=pl.BlockSpec((tm, tn), lambda i, j: (i, j)),
        out_shape=jax.ShapeDtypeStruct((n, cols), _BF16),
        scratch_shapes=[pltpu.VMEM((tm, d), _BF16)],
        compiler_params=_params(("arbitrary", "arbitrary"), est),
        name="in_proj",
    )(x2d, g1, w_in, rope_c, rope_sa, rope_sb)


def _softplus(z):
    return jnp.maximum(z, 0.0) + jnp.log1p(jnp.exp(-jnp.abs(z)))


def _rglru_kernel(xb_ref, gb_ref, cw_ref, cb_ref, wa_ref, ba_ref, wi_ref, bi_ref, l_ref, ng_ref, o_ref,
                  xf_ref, af_ref, bf_ref, ab_ref, bb_ref, *, seq, chunk):
    w = xb_ref.shape[1]
    pad = _SUBLANES
    nchunks = seq // chunk

    xf_ref[0:pad, :] = jnp.zeros((pad, w), _F32)
    xf_ref[seq + pad:seq + 2 * pad, :] = jnp.zeros((pad, w), _F32)

    def fill(c, carry):
        t0 = pl.multiple_of(c * chunk, chunk)
        xf_ref[pl.ds(t0 + pad, chunk), :] = xb_ref[pl.ds(t0, chunk), :].astype(_F32)
        return carry

    lax.fori_loop(0, nchunks, fill, 0)

    cw = cw_ref[...]
    cb = cb_ref[...]
    sp = _softplus(-l_ref[...])
    a_refs = (af_ref, ab_ref)
    b_refs = (bf_ref, bb_ref)

    def gates(c, carry):
        t0 = pl.multiple_of(c * chunk, chunk)
        xwin = xf_ref[pl.ds(t0, chunk + 2 * pad), :]
        xc = cb
        for k in range(_CONV_WIDTH):
            lo = pad - _CONV_LEFT + k
            xc = xc + xwin[lo:lo + chunk, :] * cw[k:k + 1, :]
        xcb = xc.astype(_BF16)
        for d in range(2):
            r = jax.nn.sigmoid(jnp.dot(xcb, wa_ref[d], preferred_element_type=_F32) + ba_ref[d:d + 1, :])
            i = jax.nn.sigmoid(jnp.dot(xcb, wi_ref[d], preferred_element_type=_F32) + bi_ref[d:d + 1, :])
            log_a = (-_LRU_C) * r * sp[d:d + 1, :]
            a = jnp.exp(log_a)
            a_refs[d][pl.ds(t0, chunk), :] = a
            b_refs[d][pl.ds(t0, chunk), :] = jnp.sqrt(jnp.tanh(-log_a) * (1.0 + a * a)) * (i * xc)
        return carry

    lax.fori_loop(0, nchunks, gates, 0)

    row = lax.broadcasted_iota(jnp.int32, (_SUBLANES, w), 0)
    ntiles = seq // _SUBLANES

    def scan(j, carry):
        cf, cbk = carry
        tf = pl.multiple_of(j * _SUBLANES, _SUBLANES)
        tb = pl.multiple_of((ntiles - 1 - j) * _SUBLANES, _SUBLANES)
        a = af_ref[pl.ds(tf, _SUBLANES), :]
        b = bf_ref[pl.ds(tf, _SUBLANES), :]
        a2 = ab_ref[pl.ds(tb, _SUBLANES), :]
        b2 = bb_ref[pl.ds(tb, _SUBLANES), :]
        for s in (1, 2, 4):
            keep = row >= s
            a_s = jnp.where(keep, pltpu.roll(a, s, 0), 1.0)
            b_s = jnp.where(keep, pltpu.roll(b, s, 0), 0.0)
            b = a * b_s + b
            a = a * a_s
            keep2 = row < _SUBLANES - s
            a2_s = jnp.where(keep2, pltpu.roll(a2, _SUBLANES - s, 0), 1.0)
            b2_s = jnp.where(keep2, pltpu.roll(b2, _SUBLANES - s, 0), 0.0)
            b2 = a2 * b2_s + b2
            a2 = a2 * a2_s
        h = a * cf + b
        h2 = a2 * cbk + b2
        bf_ref[pl.ds(tf, _SUBLANES), :] = h
        bb_ref[pl.ds(tb, _SUBLANES), :] = h2
        cf = jnp.broadcast_to(h[_SUBLANES - 1:_SUBLANES, :], (_SUBLANES, w))
        cbk = jnp.broadcast_to(h2[0:1, :], (_SUBLANES, w))
        return cf, cbk

    zero = jnp.zeros((_SUBLANES, w), _F32)
    lax.fori_loop(0, ntiles, scan, (zero, zero), unroll=4)

    ng = ng_ref[...]

    def finish(c, carry):
        t0 = pl.multiple_of(c * chunk, chunk)
        gate = jax.nn.gelu(gb_ref[pl.ds(t0, chunk), :].astype(_F32), approximate=True)
        y = (bf_ref[pl.ds(t0, chunk), :] + bb_ref[pl.ds(t0, chunk), :]) * gate
        o_ref[pl.ds(t0, chunk), :] = _rms_scale(y, ng).astype(o_ref.dtype)
        return carry

    lax.fori_loop(0, nchunks, finish, 0)


def _rglru(u, conv_w, conv_b, wa, ba, wi, bi, lam, norm_g, *, batch, seq, lru_w):
    nblk = lru_w // _LRU_BLOCK
    w = _LRU_BLOCK
    chunk = min(256, seq)
    est = 6 * seq * w * 2 + (seq + 16) * w * 4 + 4 * seq * w * 4 + 8 * w * w * 2 + 16 * chunk * w * 4
    blk = lambda shape, imap: pl.BlockSpec(shape, imap)
    return pl.pallas_call(
        functools.partial(_rglru_kernel, seq=seq, chunk=chunk),
        grid=(batch, nblk),
        in_specs=[
            blk((seq, w), lambda b, n: (b, n)),
            blk((seq, w), lambda b, n: (b, nblk + n)),
            blk((_CONV_WIDTH, w), lambda b, n: (0, n)),
            blk((1, w), lambda b, n: (0, n)),
            blk((2, None, w, w), lambda b, n: (0, n, 0, 0)),
            blk((2, w), lambda b, n: (0, n)),
            blk((2, None, w, w), lambda b, n: (0, n, 0, 0)),
            blk((2, w), lambda b, n: (0, n)),
            blk((2, w), lambda b, n: (0, n)),
            blk((1, w), lambda b, n: (0, n)),
        ],
        out_specs=blk((seq, w), lambda b, n: (b, n)),
        out_shape=jax.ShapeDtypeStruct((batch * seq, lru_w), _BF16),
        scratch_shapes=[pltpu.VMEM((seq + 2 * _SUBLANES, w), _F32)] + [pltpu.VMEM((seq, w), _F32)] * 4,
        compiler_params=_params(("arbitrary", "arbitrary"), est),
        name="rglru",
    )(u, u, conv_w, conv_b, wa, ba, wi, bi, lam, norm_g)


def _diff_attn_kernel(q_ref, k_ref, v_ref, lp_ref, g_ref, o_ref, *, sub):
    lp = lp_ref[...]
    lam = (jnp.exp(jnp.sum(lp[0:1] * lp[1:2], axis=-1, keepdims=True))
           - jnp.exp(jnp.sum(lp[2:3] * lp[3:4], axis=-1, keepdims=True)) + _LAMBDA_INIT)
    nt = (((1,), (1,)), ((), ()))
    gain = g_ref[...] * (1.0 - _LAMBDA_INIT)
    for t in range(q_ref.shape[0] // sub):
        rows = slice(t * sub, (t + 1) * sub)
        parts = []
        for c in range(2):
            qc = q_ref[rows, c * _HEAD_DIM:(c + 1) * _HEAD_DIM]
            kc = k_ref[:, c * _HEAD_DIM:(c + 1) * _HEAD_DIM]
            s = lax.dot_general(qc, kc, nt, preferred_element_type=_F32)
            e = jnp.exp2(s - jnp.max(s, axis=-1, keepdims=True))
            r = 1.0 / jnp.sum(e, axis=-1, keepdims=True)
            parts.append(jnp.dot(e.astype(_BF16), v_ref[...], preferred_element_type=_F32) * r)
        o = parts[0] - lam * parts[1]
        o_ref[rows, :] = _rms_scale(o, gain).astype(o_ref.dtype)


def _diff_attn(u, lam_params, subln_g, *, batch, seq, lru_w, att_w):
    heads = att_w // _V_DIM
    tq = min(1024, seq)
    qblocks = seq // tq
    q0 = 2 * lru_w // _V_DIM
    k0 = q0 + heads
    v0 = k0 + heads
    est = 2 * tq * _V_DIM * 2 + 4 * seq * _V_DIM * 2 + 2 * tq * _V_DIM * 2 + 6 * tq * seq * 4
    return pl.pallas_call(
        functools.partial(_diff_attn_kernel, sub=min(256, tq)),
        grid=(batch, heads, qblocks),
        in_specs=[
            pl.BlockSpec((tq, _V_DIM), lambda b, h, i: (b * qblocks + i, q0 + h)),
            pl.BlockSpec((seq, _V_DIM), lambda b, h, i: (b, k0 + h)),
            pl.BlockSpec((seq, _V_DIM), lambda b, h, i: (b, v0 + h)),
            pl.BlockSpec((4, _HEAD_DIM), lambda b, h, i: (0, 0)),
            pl.BlockSpec((1, _V_DIM), lambda b, h, i: (0, 0)),
        ],
        out_specs=pl.BlockSpec((tq, _V_DIM), lambda b, h, i: (b * qblocks + i, h)),
        out_shape=jax.ShapeDtypeStruct((batch * seq, att_w), _BF16),
        compiler_params=_params(("arbitrary", "arbitrary", "arbitrary"), est),
        name="diff_attn",
    )(u, u, u, lam_params, subln_g)


def _out_proj_kernel(x_ref, a_ref, b_ref, wa_ref, wb_ref, o_ref):
    acc = jnp.dot(a_ref[...], wa_ref[...], preferred_element_type=_F32)
    acc = acc + jnp.dot(b_ref[...], wb_ref[...], preferred_element_type=_F32)
    o_ref[...] = x_ref[...] + acc


def _out_proj(x2d, y_rec, y_att, w_out):
    n, d = x2d.shape
    ka, kb = y_rec.shape[1], y_att.shape[1]
    tm = min(512, n)
    tn = min(1024, d)
    kblocks_a = ka // kb
    est = 2 * tm * tn * 4 * 2 + 2 * tm * (ka + kb) * 2 + 2 * (ka + kb) * tn * 2 + tm * tn * 4
    return pl.pallas_call(
        _out_proj_kernel,
        grid=(n // tm, d // tn),
        in_specs=[
            pl.BlockSpec((tm, tn), lambda i, j: (i, j)),
            pl.BlockSpec((tm, ka), lambda i, j: (i, 0)),
            pl.BlockSpec((tm, kb), lambda i, j: (i, 0)),
            pl.BlockSpec((ka, tn), lambda i, j: (0, j)),
            pl.BlockSpec((kb, tn), lambda i, j: (kblocks_a, j)),
        ],
        out_specs=pl.BlockSpec((tm, tn), lambda i, j: (i, j)),
        out_shape=jax.ShapeDtypeStruct((n, d), _F32),
        compiler_params=_params(("arbitrary", "arbitrary"), est),
        name="out_proj",
    )(x2d, y_rec, y_att, w_out, w_out)


def _router_kernel(x_ref, g_ref, rw_ref, o_ref):
    hn = _rms_scale(x_ref[...], g_ref[...]).astype(_BF16)
    o_ref[...] = lax.dot_general(rw_ref[...], hn, (((1,), (1,)), ((), ())), preferred_element_type=_F32)


def _router_logits(x2d, g2, rw_t):
    n, d = x2d.shape
    e = rw_t.shape[0]
    tm = min(512, n)
    est = 2 * tm * d * 4 + tm * d * 2 + 2 * e * d * 2 + 2 * e * tm * 4 + tm * d * 4
    return pl.pallas_call(
        _router_kernel,
        grid=(n // tm,),
        in_specs=[
            pl.BlockSpec((tm, d), lambda i: (i, 0)),
            pl.BlockSpec((1, d), lambda i: (0, 0)),
            pl.BlockSpec((e, d), lambda i: (0, 0)),
        ],
        out_specs=pl.BlockSpec((e, tm), lambda i: (0, i)),
        out_shape=jax.ShapeDtypeStruct((e, n), _F32),
        compiler_params=_params(("arbitrary",), est),
        name="router",
    )(x2d, g2, rw_t)


def _count(mask_f32):
    return jnp.sum(jnp.sum(mask_f32, axis=0, keepdims=True), axis=1, keepdims=True)


def _route_kernel(lg_ref, lgt_ref, idx_ref, gate_ref, aff_ref, afft_ref, *, nblocks, cap):
    n_exp, nbp, _ = lg_ref.shape

    def softmax0(l):
        ex = jnp.exp(l - jnp.max(l, axis=0, keepdims=True))
        return ex / jnp.sum(ex, axis=0, keepdims=True)

    aff_ref[...] = softmax0(lg_ref[...])
    afft_ref[...] = softmax0(lgt_ref[...])

    f32 = lambda m: jnp.where(m, 1.0, 0.0).astype(_F32)
    bf = lambda x: x.astype(_BF16)
    mm = lambda a, b: jnp.dot(a, b, preferred_element_type=_F32)

    ii = lambda shape, dim: lax.broadcasted_iota(jnp.int32, shape, dim)
    u_incl = bf(f32(ii((_LANES, _LANES), 0) <= ii((_LANES, _LANES), 1)))
    l_incl = bf(f32(ii((_LANES, _LANES), 1) <= ii((_LANES, _LANES), 0)))
    l_strict = bf(f32(ii((nbp, nbp), 1) < ii((nbp, nbp), 0)))
    u_strict = bf(f32(ii((nbp, nbp), 0) < ii((nbp, nbp), 1)))
    ones_ll = jnp.ones((_LANES, _LANES), _BF16)
    ones_8l = jnp.ones((_SUBLANES, _LANES), _BF16)
    ones_8b = jnp.ones((_SUBLANES, nbp), _BF16)
    blk_a = ii((nbp, _LANES), 0)
    blk_t = ii((_LANES, nbp), 1)
    p_row = ii((1, cap), 1).astype(_F32)
    blk_p = ii((nbp, cap), 0).astype(_F32)
    lane_p = ii((_LANES, cap), 0).astype(_F32)

    def per_expert(e, carry):
        a = aff_ref[e]
        at = afft_ref[e]
        bits = jnp.where(blk_a < nblocks, pltpu.bitcast(a, jnp.int32), -1)
        bits_t = jnp.where(blk_t < nblocks, pltpu.bitcast(at, jnp.int32), -1)

        def bit_step(i, thr):
            cand = thr | lax.shift_left(jnp.int32(1), 30 - i)
            return jnp.where(_count(f32(bits >= cand)) >= cap, cand, thr)

        thr = lax.fori_loop(0, 31, bit_step, jnp.zeros((1, 1), jnp.int32))
        gt, tie = f32(bits > thr), f32(bits == thr)
        gt_t, tie_t = f32(bits_t > thr), f32(bits_t == thr)
        need = cap - _count(gt)

        tie_b = bf(tie)
        rank = mm(l_strict, bf(mm(tie_b, ones_ll))) + mm(tie_b, u_incl) - tie
        sel = bf(jnp.where(rank < need, tie, 0.0) + gt)
        tie_tb = bf(tie_t)
        rank_t = mm(bf(mm(ones_8l, tie_tb)), u_strict)[0:1, :] + mm(l_incl, tie_tb) - tie_t
        sel_t = bf(jnp.where(rank_t < need, tie_t, 0.0) + gt_t)

        tot_b = mm(sel, ones_ll)
        cend = mm(l_strict, bf(tot_b)) + tot_b
        cl_t = bf(mm(l_incl, sel_t))
        tot_row = bf(mm(ones_8l, sel_t))

        ind_le = bf(f32(cend[:, 0:1] <= p_row))
        blk_of_p = mm(ones_8b, ind_le)[0:1, :]
        p_local = p_row - mm(tot_row, ind_le)[0:1, :]
        onehot = bf(f32(blk_p == blk_of_p))
        cl_of_p = mm(cl_t, onehot)
        off = mm(ones_8l, bf(f32(cl_of_p <= p_local)))[0:1, :]
        idx_ref[e] = (blk_of_p * _LANES + off).astype(jnp.int32)

        hi = bf(at)
        r1 = at - hi.astype(_F32)
        mid = bf(r1)
        lo = bf(r1 - mid.astype(_F32))
        g_of_p = mm(hi, onehot) + mm(mid, onehot) + mm(lo, onehot)
        gate_ref[e] = jnp.sum(jnp.where(lane_p == off, g_of_p, 0.0), axis=0, keepdims=True)
        return carry

    lax.fori_loop(0, n_exp, per_expert, 0)


def _route(logits_t, *, cap):
    e, n = logits_t.shape
    nblocks = n // _LANES
    nbp = max(nblocks, _LANES)
    lg = logits_t.reshape(e, nblocks, _LANES)
    lg = jnp.pad(lg, ((0, 0), (0, nbp - nblocks), (0, 0)))
    lgt = jnp.swapaxes(lg, 1, 2)
    est = 6 * e * nbp * _LANES * 4 + 12 * max(nbp, _LANES) * cap * 4
    full = lambda shape: pl.BlockSpec(shape, lambda: (0,) * len(shape))
    idx, gate = pl.pallas_call(
        functools.partial(_route_kernel, nblocks=nblocks, cap=cap),
        in_specs=[full((e, nbp, _LANES)), full((e, _LANES, nbp))],
        out_specs=[full((e, 1, cap)), full((e, 1, cap))],
        out_shape=[jax.ShapeDtypeStruct((e, 1, cap), jnp.int32), jax.ShapeDtypeStruct((e, 1, cap), _F32)],
        scratch_shapes=[pltpu.VMEM((e, nbp, _LANES), _F32), pltpu.VMEM((e, _LANES, nbp), _F32)],
        compiler_params=pltpu.CompilerParams(vmem_limit_bytes=_vmem_limit(est)),
        name="route",
    )(lg, lgt)
    return idx.reshape(e * cap), gate.reshape(e * cap, 1)


def _ffn_up_kernel(idx_ref, x_hbm, g_ref, wg_ref, wu_ref, o_ref, xbuf, hn_ref, sem, *, tm):
    g = pl.program_id(0)
    slot = lax.rem(g, 2)

    def rows(tile, buf_slot, go):
        base = tile * tm

        def body(r, carry):
            cp = pltpu.make_async_copy(x_hbm.at[pl.ds(idx_ref[base + r], 1)],
                                       xbuf.at[buf_slot, pl.ds(r, 1)], sem.at[buf_slot])
            cp.start() if go else cp.wait()
            return carry

        lax.fori_loop(0, tm, body, 0, unroll=8)

    @pl.when(pl.program_id(1) == 0)
    def _():
        @pl.when(g == 0)
        def _():
            rows(0, 0, True)

        rows(g, slot, False)

        @pl.when(g + 1 < pl.num_programs(0))
        def _():
            rows(g + 1, 1 - slot, True)

        hn_ref[...] = _rms_scale(xbuf[slot], g_ref[...]).astype(_BF16)

    hn = hn_ref[...]
    gate = jnp.dot(hn, wg_ref[...], preferred_element_type=_F32)
    up = jnp.dot(hn, wu_ref[...], preferred_element_type=_F32)
    o_ref[...] = (jax.nn.silu(gate) * up).astype(o_ref.dtype)


def _ffn_up(idx, x2d, g2, w_gate, w_up, *, cap):
    slots = idx.shape[0]
    d = x2d.shape[1]
    ff = w_gate.shape[2]
    tm = min(512, cap)
    tf = min(512, ff)
    per_e = cap // tm
    est = 2 * tm * d * 4 + tm * d * 2 + 4 * d * tf * 2 + 2 * tm * tf * 2 + 3 * tm * tf * 4 + tm * d * 4
    return pl.pallas_call(
        functools.partial(_ffn_up_kernel, tm=tm),
        grid_spec=pltpu.PrefetchScalarGridSpec(
            num_scalar_prefetch=1,
            grid=(slots // tm, ff // tf),
            in_specs=[
                pl.BlockSpec(memory_space=pl.ANY),
                pl.BlockSpec((1, d), lambda g, f, idx_ref: (0, 0)),
                pl.BlockSpec((None, d, tf), lambda g, f, idx_ref: (g // per_e, 0, f)),
                pl.BlockSpec((None, d, tf), lambda g, f, idx_ref: (g // per_e, 0, f)),
            ],
            out_specs=pl.BlockSpec((tm, tf), lambda g, f, idx_ref: (g, f)),
            scratch_shapes=[pltpu.VMEM((2, tm, d), _F32), pltpu.VMEM((tm, d), _BF16), pltpu.SemaphoreType.DMA((2,))],
        ),
        out_shape=jax.ShapeDtypeStruct((slots, ff), _BF16),
        compiler_params=_params(("arbitrary", "arbitrary"), est),
        name="ffn_up",
    )(idx, x2d, g2, w_gate, w_up)


def _ffn_down_kernel(h_ref, w_ref, gate_ref, o_ref):
    o_ref[...] = jnp.dot(h_ref[...], w_ref[...], preferred_element_type=_F32) * gate_ref[...]


def _ffn_down(hid, w_down, gates, *, cap):
    slots, ff = hid.shape
    d = w_down.shape[2]
    tm = min(512, cap)
    tn = min(1024, d)
    per_e = cap // tm
    est = 2 * tm * ff * 2 + 2 * ff * tn * 2 + 2 * tm * tn * 4 + 2 * tm * _LANES * 4 + tm * tn * 4
    return pl.pallas_call(
        _ffn_down_kernel,
        grid=(slots // tm, d // tn),
        in_specs=[
            pl.BlockSpec((tm, ff), lambda g, j: (g, 0)),
            pl.BlockSpec((None, ff, tn), lambda g, j: (g // per_e, 0, j)),
            pl.BlockSpec((tm, 1), lambda g, j: (g, 0)),
        ],
        out_specs=pl.BlockSpec((tm, tn), lambda g, j: (g, j)),
        out_shape=jax.ShapeDtypeStruct((slots, d), _F32),
        compiler_params=_params(("arbitrary", "arbitrary"), est),
        name="ffn_down",
    )(hid, w_down, gates)


def _combine_kernel(idx_ref, x_hbm, y_ref, o_hbm, buf, sem_in, sem_out, *, rows, per_e):
    del x_hbm
    g = pl.program_id(0)
    slot = lax.rem(g, 2)
    pos = lax.rem(g, per_e)
    first = pos == 0
    last = pos == per_e - 1

    def rows_of(tile, buf_slot, fetch, go):
        base = tile * rows

        def body(r, carry):
            hbm_row = o_hbm.at[pl.ds(idx_ref[base + r], 1)]
            vmem_row = buf.at[buf_slot, pl.ds(r, 1)]
            cp = (pltpu.make_async_copy(hbm_row, vmem_row, sem_in.at[buf_slot]) if fetch
                  else pltpu.make_async_copy(vmem_row, hbm_row, sem_out.at[buf_slot]))
            cp.start() if go else cp.wait()
            return carry

        lax.fori_loop(0, rows, body, 0, unroll=8)

    @pl.when(first)
    def _():
        rows_of(g, slot, True, True)

    rows_of(g, slot, True, False)

    @pl.when(jnp.logical_not(first))
    def _():
        rows_of(g - 1, 1 - slot, False, False)

    @pl.when(jnp.logical_not(last))
    def _():
        rows_of(g + 1, 1 - slot, True, True)

    buf[slot] = buf[slot] + y_ref[...]
    rows_of(g, slot, False, True)

    @pl.when(last)
    def _():
        rows_of(g, slot, False, False)


def _combine(idx, x2d, outs, *, cap):
    slots, d = outs.shape
    rows = min(256, cap)
    assert cap % rows == 0
    est = 3 * rows * d * 4 + 2 * rows * d * 4
    return pl.pallas_call(
        functools.partial(_combine_kernel, rows=rows, per_e=cap // rows),
        grid_spec=pltpu.PrefetchScalarGridSpec(
            num_scalar_prefetch=1,
            grid=(slots // rows,),
            in_specs=[pl.BlockSpec(memory_space=pl.ANY),
                      pl.BlockSpec((rows, d), lambda g, idx_ref: (g, 0))],
            out_specs=pl.BlockSpec(memory_space=pl.ANY),
            scratch_shapes=[pltpu.VMEM((2, rows, d), _F32), pltpu.SemaphoreType.DMA((2,)),
                            pltpu.SemaphoreType.DMA((2,))],
        ),
        out_shape=jax.ShapeDtypeStruct(x2d.shape, x2d.dtype),
        input_output_aliases={1: 0},
        compiler_params=pltpu.CompilerParams(dimension_semantics=("arbitrary",), vmem_limit_bytes=_vmem_limit(est),
                                             has_side_effects=True),
        name="combine",
    )(idx, x2d, outs)


def _final_norm_kernel(x_ref, g_ref, o_ref):
    o_ref[...] = _rms_scale(x_ref[...], g_ref[...])


def _final_norm(x2d, g):
    n, d = x2d.shape
    tm = min(512, n)
    return pl.pallas_call(
        _final_norm_kernel,
        grid=(n // tm,),
        in_specs=[pl.BlockSpec((tm, d), lambda i: (i, 0)), pl.BlockSpec((1, d), lambda i: (0, 0))],
        out_specs=pl.BlockSpec((tm, d), lambda i: (i, 0)),
        out_shape=jax.ShapeDtypeStruct((n, d), _F32),
        compiler_params=_params(("arbitrary",), 5 * tm * d * 4),
        name="final_norm",
    )(x2d, g)


def _rope_tables(seq):
    half = _ROT_DIM // 2
    inv_freq = _ROPE_THETA ** (-jnp.arange(0, _ROT_DIM, 2, dtype=_F32) / _ROT_DIM)
    ang = jnp.arange(seq, dtype=_F32)[:, None] * inv_freq[None, :]
    cos, sin = jnp.cos(ang), jnp.sin(ang)
    zeros = lambda k: jnp.zeros((seq, k), _F32)
    c = jnp.concatenate([cos, cos, jnp.ones((seq, _LANES - _ROT_DIM), _F32)], axis=1)
    sa = jnp.concatenate([-sin, zeros(_LANES - half)], axis=1)
    sb = jnp.concatenate([zeros(half), sin, zeros(_LANES - _ROT_DIM)], axis=1)
    return c, sa, sb


def _trunk(x, p):
    batch, seq, d = x.shape
    n = batch * seq
    lru_w = p["conv_w"].shape[1]
    att_w = (p["w_in"].shape[1] - 2 * lru_w) // 3
    n_exp = p["rw_t"].shape[0]
    cap = max(1, _EC_CAPACITY_FACTOR * n // n_exp)
    x2d = x.reshape(n, d)

    u = _in_proj(x2d, p["norm1_g"], p["w_in"], *p["rope"], seq=seq, lru_w=lru_w, att_w=att_w)
    y_rec = _rglru(u, p["conv_w"], p["conv_b"], p["lru_wa"], p["lru_ba"], p["lru_wi"], p["lru_bi"], p["lru_L"],
                   p["lru_norm_g"], batch=batch, seq=seq, lru_w=lru_w)
    y_att = _diff_attn(u, p["diff_lambda"], p["subln_g"], batch=batch, seq=seq, lru_w=lru_w, att_w=att_w)
    x1 = _out_proj(x2d, y_rec, y_att, p["w_out"])

    logits_t = _router_logits(x1, p["norm2_g"], p["rw_t"])
    idx, gates = _route(logits_t, cap=cap)
    hid = _ffn_up(idx, x1, p["norm2_g"], p["w_gate"], p["w_up"], cap=cap)
    outs = _ffn_down(hid, p["w_down"], gates, cap=cap)
    x2 = _combine(idx, x1, outs, cap=cap)
    return _final_norm(x2, p["final_g"]).reshape(batch, seq, d)


def kernel(x_prompt, x_sample, norm1_g, w_in, conv_w, conv_b, lru_wa, lru_ba, lru_wi, lru_bi, lru_L, lru_norm_g,
           diff_lambda, subln_g, w_out, norm2_g, router_w, w_gate, w_up, w_down, final_g):
    assert norm1_g.shape[0] == 1, "single-layer trunk"
    row = lambda v: v.reshape(1, -1).astype(_F32)
    p = {
        "norm1_g": row(norm1_g[0]), "w_in": w_in[0].astype(_BF16),
        "conv_w": conv_w[0], "conv_b": row(conv_b[0]),
        "lru_wa": lru_wa[0].astype(_BF16), "lru_ba": lru_ba[0], "lru_wi": lru_wi[0].astype(_BF16),
        "lru_bi": lru_bi[0], "lru_L": lru_L[0], "lru_norm_g": row(lru_norm_g[0]),
        "diff_lambda": diff_lambda[0], "subln_g": row(subln_g[0]),
        "w_out": w_out[0].astype(_BF16), "norm2_g": row(norm2_g[0]),
        "rw_t": router_w[0].T.astype(_BF16),
        "w_gate": w_gate[0].astype(_BF16), "w_up": w_up[0].astype(_BF16), "w_down": w_down[0].astype(_BF16),
        "final_g": row(final_g),
        "rope": _rope_tables(x_prompt.shape[1]),
    }
    assert x_prompt.shape[1] == x_sample.shape[1], "both groups share the rotary tables"
    return _trunk(x_prompt, p), _trunk(x_sample, p)
```

```python
import functools
import math

import jax
import jax.numpy as jnp
from jax import lax
from jax.experimental import pallas as pl
from jax.experimental.pallas import tpu as pltpu

_LANES = 128
_SUBLANES = 8
_HEAD_DIM = 128
_V_DIM = 2 * _HEAD_DIM
_LRU_BLOCK = 256
_ROT_DIM = _HEAD_DIM // 4
_CONV_WIDTH = 4
_CONV_LEFT = 2
_LRU_C = 8.0
_ROPE_THETA = 500000.0
_NORM_EPS = 1e-6
_EC_CAPACITY_FACTOR = 2
_LAMBDA_INIT = 0.8 - 0.6 * math.exp(-0.3 * 0)
_V7X_VMEM_BYTES = 64 * 2**20

_F32 = jnp.float32
_BF16 = jnp.bfloat16


def _vmem_limit(estimate_bytes):
    return int(min(_V7X_VMEM_BYTES - 6 * 2**20, max(32 * 2**20, estimate_bytes + 12 * 2**20)))


def _params(semantics, estimate_bytes):
    return pltpu.CompilerParams(dimension_semantics=semantics, vmem_limit_bytes=_vmem_limit(estimate_bytes))


def _rms_scale(x, g):
    ms = jnp.mean(x * x, axis=-1, keepdims=True)
    return x * lax.rsqrt(ms + _NORM_EPS) * g


def _in_proj_kernel(x_ref, g_ref, w_ref, c_ref, sa_ref, sb_ref, o_ref, hn_ref, *, sub):
    @pl.when(pl.program_id(1) == 0)
    def _():
        hn_ref[...] = _rms_scale(x_ref[...], g_ref[...]).astype(_BF16)

    hn = hn_ref[...]
    c, sa, sb = c_ref[...], sa_ref[...], sb_ref[...]
    for lo in range(0, w_ref.shape[1], sub):
        acc = jnp.dot(hn, w_ref[:, lo:lo + sub], preferred_element_type=_F32)
        for grp in range(sub // _LANES):
            blk = acc[:, grp * _LANES:(grp + 1) * _LANES]
            rot = (blk * c
                   + pltpu.roll(blk, _LANES - _ROT_DIM // 2, 1) * sa
                   + pltpu.roll(blk, _ROT_DIM // 2, 1) * sb)
            o_ref[:, lo + grp * _LANES:lo + (grp + 1) * _LANES] = rot.astype(o_ref.dtype)


def _in_proj(x2d, g1, w_in, rope_c, rope_sa, rope_sb, *, seq, lru_w, att_w):
    n, d = x2d.shape
    cols = w_in.shape[1]
    tm = min(512, seq)
    tn = min(1024, lru_w, att_w)
    q_lo = 2 * lru_w // tn
    k_lo = q_lo + att_w // tn
    v_lo = k_lo + att_w // tn
    sblocks = seq // tm

    def table(i, j):
        kind = jnp.where(j < q_lo, 2, jnp.where(j < k_lo, 0, jnp.where(j < v_lo, 1, 2)))
        return kind, i % sblocks, 0

    est = 2 * tm * d * 4 + tm * d * 2 + 2 * d * tn * 2 + 2 * tm * tn * 2 + 6 * tm * _LANES * 4 + 2 * tm * tn * 4
    return pl.pallas_call(
        functools.partial(_in_proj_kernel, sub=min(256, tn)),
        grid=(n // tm, cols // tn),
        in_specs=[
            pl.BlockSpec((tm, d), lambda i, j: (i, 0)),
            pl.BlockSpec((1, d), lambda i, j: (0, 0)),
            pl.BlockSpec((d, tn), lambda i, j: (0, j)),
            pl.BlockSpec((None, tm, _LANES), table),
            pl.BlockSpec((None, tm, _LANES), table),
            pl.BlockSpec((None, tm, _LANES), table),
        ],
        out_specs=pl.BlockSpec((tm, tn), lambda i, j: (i, j)),
        out_shape=jax.ShapeDtypeStruct((n, cols), _BF16),
        scratch_shapes=[pltpu.VMEM((tm, d), _BF16)],
        compiler_params=_params(("arbitrary", "arbitrary"), est),
        name="in_proj",
    )(x2d, g1, w_in, rope_c, rope_sa, rope_sb)


def _softplus(z):
    return jnp.maximum(z, 0.0) + jnp.log1p(jnp.exp(-jnp.abs(z)))


def _sigmoid(z):
    return 0.5 * jnp.tanh(0.5 * z) + 0.5


def _rglru_kernel(xb_ref, gb_ref, cw_ref, cb_ref, wa_ref, ba_ref, wi_ref, bi_ref, l_ref, ng_ref, o_ref,
                  xf_ref, af_ref, bf_ref, ab_ref, bb_ref, *, seq, chunk):
    w = xb_ref.shape[1]
    pad = _SUBLANES
    nchunks = seq // chunk

    xf_ref[0:pad, :] = jnp.zeros((pad, w), _F32)
    xf_ref[seq + pad:seq + 2 * pad, :] = jnp.zeros((pad, w), _F32)

    def fill(c, carry):
        t0 = pl.multiple_of(c * chunk, chunk)
        xf_ref[pl.ds(t0 + pad, chunk), :] = xb_ref[pl.ds(t0, chunk), :].astype(_F32)
        return carry

    lax.fori_loop(0, nchunks, fill, 0)

    cw = cw_ref[...]
    cb = cb_ref[...]
    sp = _softplus(-l_ref[...])
    a_refs = (af_ref, ab_ref)
    b_refs = (bf_ref, bb_ref)

    def gates(c, carry):
        t0 = pl.multiple_of(c * chunk, chunk)
        xwin = xf_ref[pl.ds(t0, chunk + 2 * pad), :]
        xc = cb
        for k in range(_CONV_WIDTH):
            lo = pad - _CONV_LEFT + k
            xc = xc + xwin[lo:lo + chunk, :] * cw[k:k + 1, :]
        xcb = xc.astype(_BF16)
        for d in range(2):
            r = _sigmoid(jnp.dot(xcb, wa_ref[d], preferred_element_type=_F32) + ba_ref[d:d + 1, :])
            i = _sigmoid(jnp.dot(xcb, wi_ref[d], preferred_element_type=_F32) + bi_ref[d:d + 1, :])
            log_a = (-_LRU_C) * r * sp[d:d + 1, :]
            a = jnp.exp(log_a)
            a_refs[d][pl.ds(t0, chunk), :] = a
            b_refs[d][pl.ds(t0, chunk), :] = jnp.sqrt(jnp.tanh(-log_a) * (1.0 + a * a)) * (i * xc)
        return carry

    lax.fori_loop(0, nchunks, gates, 0)

    row = lax.broadcasted_iota(jnp.int32, (_SUBLANES, w), 0)
    ntiles = seq // _SUBLANES

    def scan(j, carry):
        cf, cbk = carry
        tf = pl.multiple_of(j * _SUBLANES, _SUBLANES)
        tb = pl.multiple_of((ntiles - 1 - j) * _SUBLANES, _SUBLANES)
        a = af_ref[pl.ds(tf, _SUBLANES), :]
        b = bf_ref[pl.ds(tf, _SUBLANES), :]
        a2 = ab_ref[pl.ds(tb, _SUBLANES), :]
        b2 = bb_ref[pl.ds(tb, _SUBLANES), :]
        for s in (1, 2, 4):
            keep = row >= s
            a_s = jnp.where(keep, pltpu.roll(a, s, 0), 1.0)
            b_s = jnp.where(keep, pltpu.roll(b, s, 0), 0.0)
            b = a * b_s + b
            a = a * a_s
            keep2 = row < _SUBLANES - s
            a2_s = jnp.where(keep2, pltpu.roll(a2, _SUBLANES - s, 0), 1.0)
            b2_s = jnp.where(keep2, pltpu.roll(b2, _SUBLANES - s, 0), 0.0)
            b2 = a2 * b2_s + b2
            a2 = a2 * a2_s
        h = a * cf + b
        h2 = a2 * cbk + b2
        bf_ref[pl.ds(tf, _SUBLANES), :] = h
        bb_ref[pl.ds(tb, _SUBLANES), :] = h2
        cf = jnp.broadcast_to(h[_SUBLANES - 1:_SUBLANES, :], (_SUBLANES, w))
        cbk = jnp.broadcast_to(h2[0:1, :], (_SUBLANES, w))
        return cf, cbk

    zero = jnp.zeros((_SUBLANES, w), _F32)
    lax.fori_loop(0, ntiles, scan, (zero, zero), unroll=4)

    ng = ng_ref[...]

    def finish(c, carry):
        t0 = pl.multiple_of(c * chunk, chunk)
        gate = jax.nn.gelu(gb_ref[pl.ds(t0, chunk), :].astype(_F32), approximate=True)
        y = (bf_ref[pl.ds(t0, chunk), :] + bb_ref[pl.ds(t0, chunk), :]) * gate
        o_ref[pl.ds(t0, chunk), :] = _rms_scale(y, ng).astype(o_ref.dtype)
        return carry

    lax.fori_loop(0, nchunks, finish, 0)


def _rglru(u, conv_w, conv_b, wa, ba, wi, bi, lam, norm_g, *, batch, seq, lru_w):
    nblk = lru_w // _LRU_BLOCK
    w = _LRU_BLOCK
    chunk = min(256, seq)
    est = 6 * seq * w * 2 + (seq + 16) * w * 4 + 4 * seq * w * 4 + 8 * w * w * 2 + 16 * chunk * w * 4
    blk = lambda shape, imap: pl.BlockSpec(shape, imap)
    return pl.pallas_call(
        functools.partial(_rglru_kernel, seq=seq, chunk=chunk),
        grid=(batch, nblk),
        in_specs=[
            blk((seq, w), lambda b, n: (b, n)),
            blk((seq, w), lambda b, n: (b, nblk + n)),
            blk((_CONV_WIDTH, w), lambda b, n: (0, n)),
            blk((1, w), lambda b, n: (0, n)),
            blk((2, None, w, w), lambda b, n: (0, n, 0, 0)),
            blk((2, w), lambda b, n: (0, n)),
            blk((2, None, w, w), lambda b, n: (0, n, 0, 0)),
            blk((2, w), lambda b, n: (0, n)),
            blk((2, w), lambda b, n: (0, n)),
            blk((1, w), lambda b, n: (0, n)),
        ],
        out_specs=blk((seq, w), lambda b, n: (b, n)),
        out_shape=jax.ShapeDtypeStruct((batch * seq, lru_w), _BF16),
        scratch_shapes=[pltpu.VMEM((seq + 2 * _SUBLANES, w), _F32)] + [pltpu.VMEM((seq, w), _F32)] * 4,
        compiler_params=_params(("arbitrary", "arbitrary"), est),
        name="rglru",
    )(u, u, conv_w, conv_b, wa, ba, wi, bi, lam, norm_g)


def _diff_attn_kernel(q_ref, k_ref, v_ref, lp_ref, g_ref, o_ref, *, sub):
    lp = lp_ref[...]
    lam = (jnp.exp(jnp.sum(lp[0:1] * lp[1:2], axis=-1, keepdims=True))
           - jnp.exp(jnp.sum(lp[2:3] * lp[3:4], axis=-1, keepdims=True)) + _LAMBDA_INIT)
    nt = (((1,), (1,)), ((), ()))
    gain = g_ref[...] * (1.0 - _LAMBDA_INIT)
    for t in range(q_ref.shape[0] // sub):
        rows = slice(t * sub, (t + 1) * sub)
        parts = []
        for c in range(2):
            qc = q_ref[rows, c * _HEAD_DIM:(c + 1) * _HEAD_DIM]
            kc = k_ref[:, c * _HEAD_DIM:(c + 1) * _HEAD_DIM]
            s = lax.dot_general(qc, kc, nt, preferred_element_type=_F32)
            e = jnp.exp2(s - jnp.max(s, axis=-1, keepdims=True))
            r = 1.0 / jnp.sum(e, axis=-1, keepdims=True)
            parts.append(jnp.dot(e.astype(_BF16), v_ref[...], preferred_element_type=_F32) * r)
        o = parts[0] - lam * parts[1]
        o_ref[rows, :] = _rms_scale(o, gain).astype(o_ref.dtype)


def _diff_attn(u, lam_params, subln_g, *, batch, seq, lru_w, att_w):
    heads = att_w // _V_DIM
    tq = min(1024, seq)
    qblocks = seq // tq
    q0 = 2 * lru_w // _V_DIM
    k0 = q0 + heads
    v0 = k0 + heads
    est = 2 * tq * _V_DIM * 2 + 4 * seq * _V_DIM * 2 + 2 * tq * _V_DIM * 2 + 6 * tq * seq * 4
    return pl.pallas_call(
        functools.partial(_diff_attn_kernel, sub=min(256, tq)),
        grid=(batch, heads, qblocks),
        in_specs=[
            pl.BlockSpec((tq, _V_DIM), lambda b, h, i: (b * qblocks + i, q0 + h)),
            pl.BlockSpec((seq, _V_DIM), lambda b, h, i: (b, k0 + h)),
            pl.BlockSpec((seq, _V_DIM), lambda b, h, i: (b, v0 + h)),
            pl.BlockSpec((4, _HEAD_DIM), lambda b, h, i: (0, 0)),
            pl.BlockSpec((1, _V_DIM), lambda b, h, i: (0, 0)),
        ],
        out_specs=pl.BlockSpec((tq, _V_DIM), lambda b, h, i: (b * qblocks + i, h)),
        out_shape=jax.ShapeDtypeStruct((batch * seq, att_w), _BF16),
        compiler_params=_params(("arbitrary", "arbitrary", "arbitrary"), est),
        name="diff_attn",
    )(u, u, u, lam_params, subln_g)


def _out_proj_kernel(x_ref, a_ref, b_ref, wa_ref, wb_ref, o_ref):
    acc = jnp.dot(a_ref[...], wa_ref[...], preferred_element_type=_F32)
    acc = acc + jnp.dot(b_ref[...], wb_ref[...], preferred_element_type=_F32)
    o_ref[...] = x_ref[...] + acc


def _out_proj(x2d, y_rec, y_att, w_out):
    n, d = x2d.shape
    ka, kb = y_rec.shape[1], y_att.shape[1]
    tm = min(512, n)
    tn = min(1024, d)
    kblocks_a = ka // kb
    est = 2 * tm * tn * 4 * 2 + 2 * tm * (ka + kb) * 2 + 2 * (ka + kb) * tn * 2 + tm * tn * 4
    return pl.pallas_call(
        _out_proj_kernel,
        grid=(n // tm, d // tn),
        in_specs=[
            pl.BlockSpec((tm, tn), lambda i, j: (i, j)),
            pl.BlockSpec((tm, ka), lambda i, j: (i, 0)),
            pl.BlockSpec((tm, kb), lambda i, j: (i, 0)),
            pl.BlockSpec((ka, tn), lambda i, j: (0, j)),
            pl.BlockSpec((kb, tn), lambda i, j: (kblocks_a, j)),
        ],
        out_specs=pl.BlockSpec((tm, tn), lambda i, j: (i, j)),
        out_shape=jax.ShapeDtypeStruct((n, d), _F32),
        compiler_params=_params(("arbitrary", "arbitrary"), est),
        name="out_proj",
    )(x2d, y_rec, y_att, w_out, w_out)


def _router_kernel(x_ref, g_ref, rw_ref, o_ref):
    hn = _rms_scale(x_ref[...], g_ref[...]).astype(_BF16)
    o_ref[...] = lax.dot_general(rw_ref[...], hn, (((1,), (1,)), ((), ())), preferred_element_type=_F32)


def _router_logits(x2d, g2, rw_t):
    n, d = x2d.shape
    e = rw_t.shape[0]
    tm = min(512, n)
    est = 2 * tm * d * 4 + tm * d * 2 + 2 * e * d * 2 + 2 * e * tm * 4 + tm * d * 4
    return pl.pallas_call(
        _router_kernel,
        grid=(n // tm,),
        in_specs=[
            pl.BlockSpec((tm, d), lambda i: (i, 0)),
            pl.BlockSpec((1, d), lambda i: (0, 0)),
            pl.BlockSpec((e, d), lambda i: (0, 0)),
        ],
        out_specs=pl.BlockSpec((e, tm), lambda i: (0, i)),
        out_shape=jax.ShapeDtypeStruct((e, n), _F32),
        compiler_params=_params(("arbitrary",), est),
        name="router",
    )(x2d, g2, rw_t)


def _count(mask_f32):
    return jnp.sum(jnp.sum(mask_f32, axis=0, keepdims=True), axis=1, keepdims=True)


def _route_kernel(lg_ref, lgt_ref, idx_ref, gate_ref, aff_ref, afft_ref, *, nblocks, cap):
    n_exp, nbp, _ = lg_ref.shape

    def softmax0(l):
        ex = jnp.exp(l - jnp.max(l, axis=0, keepdims=True))
        return ex / jnp.sum(ex, axis=0, keepdims=True)

    aff_ref[...] = softmax0(lg_ref[...])
    afft_ref[...] = softmax0(lgt_ref[...])

    f32 = lambda m: jnp.where(m, 1.0, 0.0).astype(_F32)
    bf = lambda x: x.astype(_BF16)
    mm = lambda a, b: jnp.dot(a, b, preferred_element_type=_F32)

    ii = lambda shape, dim: lax.broadcasted_iota(jnp.int32, shape, dim)
    u_incl = bf(f32(ii((_LANES, _LANES), 0) <= ii((_LANES, _LANES), 1)))
    l_incl = bf(f32(ii((_LANES, _LANES), 1) <= ii((_LANES, _LANES), 0)))
    l_strict = bf(f32(ii((nbp, nbp), 1) < ii((nbp, nbp), 0)))
    u_strict = bf(f32(ii((nbp, nbp), 0) < ii((nbp, nbp), 1)))
    ones_ll = jnp.ones((_LANES, _LANES), _BF16)
    ones_8l = jnp.ones((_SUBLANES, _LANES), _BF16)
    ones_8b = jnp.ones((_SUBLANES, nbp), _BF16)
    blk_a = ii((nbp, _LANES), 0)
    blk_t = ii((_LANES, nbp), 1)
    p_row = ii((1, cap), 1).astype(_F32)
    blk_p = ii((nbp, cap), 0).astype(_F32)
    lane_p = ii((_LANES, cap), 0).astype(_F32)

    def per_expert(e, carry):
        a = aff_ref[e]
        at = afft_ref[e]
        bits = jnp.where(blk_a < nblocks, pltpu.bitcast(a, jnp.int32), -1)
        bits_t = jnp.where(blk_t < nblocks, pltpu.bitcast(at, jnp.int32), -1)

        def bit_step(i, thr):
            cand = thr | lax.shift_left(jnp.int32(1), 30 - i)
            return jnp.where(_count(f32(bits >= cand)) >= cap, cand, thr)

        thr = lax.fori_loop(0, 31, bit_step, jnp.zeros((1, 1), jnp.int32))
        gt, tie = f32(bits > thr), f32(bits == thr)
        gt_t, tie_t = f32(bits_t > thr), f32(bits_t == thr)
        need = cap - _count(gt)

        tie_b = bf(tie)
        rank = mm(l_strict, bf(mm(tie_b, ones_ll))) + mm(tie_b, u_incl) - tie
        sel = bf(jnp.where(rank < need, tie, 0.0) + gt)
        tie_tb = bf(tie_t)
        rank_t = mm(bf(mm(ones_8l, tie_tb)), u_strict)[0:1, :] + mm(l_incl, tie_tb) - tie_t
        sel_t = bf(jnp.where(rank_t < need, tie_t, 0.0) + gt_t)

        tot_b = mm(sel, ones_ll)
        cend = mm(l_strict, bf(tot_b)) + tot_b
        cl_t = bf(mm(l_incl, sel_t))
        tot_row = bf(mm(ones_8l, sel_t))

        ind_le = bf(f32(cend[:, 0:1] <= p_row))
        blk_of_p = mm(ones_8b, ind_le)[0:1, :]
        p_local = p_row - mm(tot_row, ind_le)[0:1, :]
        onehot = bf(f32(blk_p == blk_of_p))
        cl_of_p = mm(cl_t, onehot)
        off = mm(ones_8l, bf(f32(cl_of_p <= p_local)))[0:1, :]
        idx_ref[e] = (blk_of_p * _LANES + off).astype(jnp.int32)

        hi = bf(at)
        r1 = at - hi.astype(_F32)
        mid = bf(r1)
        lo = bf(r1 - mid.astype(_F32))
        g_of_p = mm(hi, onehot) + mm(mid, onehot) + mm(lo, onehot)
        gate_ref[e] = jnp.sum(jnp.where(lane_p == off, g_of_p, 0.0), axis=0, keepdims=True)
        return carry

    lax.fori_loop(0, n_exp, per_expert, 0)


def _route(logits_t, *, cap):
    e, n = logits_t.shape
    nblocks = n // _LANES
    nbp = max(nblocks, _LANES)
    lg = logits_t.reshape(e, nblocks, _LANES)
    lg = jnp.pad(lg, ((0, 0), (0, nbp - nblocks), (0, 0)))
    lgt = jnp.swapaxes(lg, 1, 2)
    est = 6 * e * nbp * _LANES * 4 + 12 * max(nbp, _LANES) * cap * 4
    full = lambda shape: pl.BlockSpec(shape, lambda: (0,) * len(shape))
    idx, gate = pl.pallas_call(
        functools.partial(_route_kernel, nblocks=nblocks, cap=cap),
        in_specs=[full((e, nbp, _LANES)), full((e, _LANES, nbp))],
        out_specs=[full((e, 1, cap)), full((e, 1, cap))],
        out_shape=[jax.ShapeDtypeStruct((e, 1, cap), jnp.int32), jax.ShapeDtypeStruct((e, 1, cap), _F32)],
        scratch_shapes=[pltpu.VMEM((e, nbp, _LANES), _F32), pltpu.VMEM((e, _LANES, nbp), _F32)],
        compiler_params=pltpu.CompilerParams(vmem_limit_bytes=_vmem_limit(est)),
        name="route",
    )(lg, lgt)
    return idx.reshape(e * cap), gate.reshape(e * cap, 1)


def _ffn_up_kernel(idx_ref, x_hbm, g_ref, wg_ref, wu_ref, o_ref, xbuf, hn_ref, sem, *, tm):
    g, f = pl.program_id(0), pl.program_id(1)
    n_tiles, n_f = pl.num_programs(0), pl.num_programs(1)
    slot = lax.rem(g, 2)
    share = tm // n_f

    def row_copy(tile, buf_slot, r):
        return pltpu.make_async_copy(x_hbm.at[pl.ds(idx_ref[tile * tm + r], 1)],
                                     xbuf.at[buf_slot, pl.ds(r, 1)], sem.at[buf_slot])

    def all_rows(tile, buf_slot, go):
        def body(r, carry):
            cp = row_copy(tile, buf_slot, r)
            cp.start() if go else cp.wait()
            return carry

        lax.fori_loop(0, tm, body, 0, unroll=8)

    @pl.when(f == 0)
    def _():
        @pl.when(g == 0)
        def _():
            all_rows(0, 0, True)

        all_rows(g, slot, False)
        hn_ref[...] = _rms_scale(xbuf[slot], g_ref[...]).astype(_BF16)

    nxt = jnp.minimum(g + 1, n_tiles - 1)
    for r in range(share):
        row_copy(nxt, 1 - slot, f * share + r).start()

    hn = hn_ref[...]
    gate = jnp.dot(hn, wg_ref[...], preferred_element_type=_F32)
    up = jnp.dot(hn, wu_ref[...], preferred_element_type=_F32)
    o_ref[...] = (jax.nn.silu(gate) * up).astype(o_ref.dtype)

    @pl.when(jnp.logical_and(g == n_tiles - 1, f == n_f - 1))
    def _():
        all_rows(nxt, 1 - slot, False)


def _ffn_up(idx, x2d, g2, w_gate, w_up, *, cap):
    slots = idx.shape[0]
    d = x2d.shape[1]
    ff = w_gate.shape[2]
    tm = min(512, cap)
    tf = min(512, ff)
    per_e = cap // tm
    est = 2 * tm * d * 4 + tm * d * 2 + 4 * d * tf * 2 + 2 * tm * tf * 2 + 3 * tm * tf * 4 + tm * d * 4
    return pl.pallas_call(
        functools.partial(_ffn_up_kernel, tm=tm),
        grid_spec=pltpu.PrefetchScalarGridSpec(
            num_scalar_prefetch=1,
            grid=(slots // tm, ff // tf),
            in_specs=[
                pl.BlockSpec(memory_space=pl.ANY),
                pl.BlockSpec((1, d), lambda g, f, idx_ref: (0, 0)),
                pl.BlockSpec((None, d, tf), lambda g, f, idx_ref: (g // per_e, 0, f)),
                pl.BlockSpec((None, d, tf), lambda g, f, idx_ref: (g // per_e, 0, f)),
            ],
            out_specs=pl.BlockSpec((tm, tf), lambda g, f, idx_ref: (g, f)),
            scratch_shapes=[pltpu.VMEM((2, tm, d), _F32), pltpu.VMEM((tm, d), _BF16), pltpu.SemaphoreType.DMA((2,))],
        ),
        out_shape=jax.ShapeDtypeStruct((slots, ff), _BF16),
        compiler_params=_params(("arbitrary", "arbitrary"), est),
        name="ffn_up",
    )(idx, x2d, g2, w_gate, w_up)


def _ffn_down_kernel(idx_ref, x_hbm, h_ref, w_ref, gate_ref, o_hbm, buf, sem_in, sem_out, *, rows, per_e):
    del x_hbm
    g = pl.program_id(0)
    slot = lax.rem(g, 2)
    pos = lax.rem(g, per_e)
    first = pos == 0
    last = pos == per_e - 1

    def rows_of(tile, buf_slot, fetch, go):
        base = tile * rows

        def body(r, carry):
            hbm_row = o_hbm.at[pl.ds(idx_ref[base + r], 1)]
            vmem_row = buf.at[buf_slot, pl.ds(r, 1)]
            cp = (pltpu.make_async_copy(hbm_row, vmem_row, sem_in.at[buf_slot]) if fetch
                  else pltpu.make_async_copy(vmem_row, hbm_row, sem_out.at[buf_slot]))
            cp.start() if go else cp.wait()
            return carry

        lax.fori_loop(0, rows, body, 0, unroll=8)

    @pl.when(first)
    def _():
        rows_of(g, slot, True, True)

    rows_of(g, slot, True, False)

    @pl.when(jnp.logical_not(first))
    def _():
        rows_of(g - 1, 1 - slot, False, False)

    @pl.when(jnp.logical_not(last))
    def _():
        rows_of(g + 1, 1 - slot, True, True)

    y = jnp.dot(h_ref[...], w_ref[...], preferred_element_type=_F32) * gate_ref[...]
    buf[slot] = buf[slot] + y
    rows_of(g, slot, False, True)

    @pl.when(last)
    def _():
        rows_of(g, slot, False, False)


def _ffn_down(idx, x2d, hid, w_down, gates, *, cap):
    slots, ff = hid.shape
    d = w_down.shape[2]
    rows = min(256, cap)
    assert cap % rows == 0
    per_e = cap // rows
    est = 2 * ff * d * 2 + 2 * rows * ff * 2 + 2 * rows * d * 4 + 2 * rows * d * 4 + 2 * rows * _LANES * 4
    return pl.pallas_call(
        functools.partial(_ffn_down_kernel, rows=rows, per_e=per_e),
        grid_spec=pltpu.PrefetchScalarGridSpec(
            num_scalar_prefetch=1,
            grid=(slots // rows,),
            in_specs=[pl.BlockSpec(memory_space=pl.ANY),
                      pl.BlockSpec((rows, ff), lambda g, idx_ref: (g, 0)),
                      pl.BlockSpec((None, ff, d), lambda g, idx_ref: (g // per_e, 0, 0)),
                      pl.BlockSpec((rows, 1), lambda g, idx_ref: (g, 0))],
            out_specs=pl.BlockSpec(memory_space=pl.ANY),
            scratch_shapes=[pltpu.VMEM((2, rows, d), _F32), pltpu.SemaphoreType.DMA((2,)),
                            pltpu.SemaphoreType.DMA((2,))],
        ),
        out_shape=jax.ShapeDtypeStruct(x2d.shape, x2d.dtype),
        input_output_aliases={1: 0},
        compiler_params=pltpu.CompilerParams(dimension_semantics=("arbitrary",), vmem_limit_bytes=_vmem_limit(est),
                                             has_side_effects=True),
        name="ffn_down",
    )(idx, x2d, hid, w_down, gates)


def _final_norm_kernel(x_ref, g_ref, o_ref):
    o_ref[...] = _rms_scale(x_ref[...], g_ref[...])


def _final_norm(x2d, g):
    n, d = x2d.shape
    tm = min(512, n)
    return pl.pallas_call(
        _final_norm_kernel,
        grid=(n // tm,),
        in_specs=[pl.BlockSpec((tm, d), lambda i: (i, 0)), pl.BlockSpec((1, d), lambda i: (0, 0))],
        out_specs=pl.BlockSpec((tm, d), lambda i: (i, 0)),
        out_shape=jax.ShapeDtypeStruct((n, d), _F32),
        compiler_params=_params(("arbitrary",), 5 * tm * d * 4),
        name="final_norm",
    )(x2d, g)


def _rope_tables(seq):
    half = _ROT_DIM // 2
    inv_freq = _ROPE_THETA ** (-jnp.arange(0, _ROT_DIM, 2, dtype=_F32) / _ROT_DIM)
    ang = jnp.arange(seq, dtype=_F32)[:, None] * inv_freq[None, :]
    cos, sin = jnp.cos(ang), jnp.sin(ang)
    zeros = lambda k: jnp.zeros((seq, k), _F32)
    c = jnp.concatenate([cos, cos, jnp.ones((seq, _LANES - _ROT_DIM), _F32)], axis=1)
    sa = jnp.concatenate([-sin, zeros(_LANES - half)], axis=1)
    sb = jnp.concatenate([zeros(half), sin, zeros(_LANES - _ROT_DIM)], axis=1)
    q_scale = _HEAD_DIM ** -0.5 * math.log2(math.e)
    stack = lambda t, ident: jnp.stack([t * q_scale, t, jnp.full_like(t, ident)])
    return stack(c, 1.0), stack(sa, 0.0), stack(sb, 0.0)


def _trunk(x, p):
    batch, seq, d = x.shape
    n = batch * seq
    lru_w = p["conv_w"].shape[1]
    att_w = (p["w_in"].shape[1] - 2 * lru_w) // 3
    n_exp = p["rw_t"].shape[0]
    cap = max(1, _EC_CAPACITY_FACTOR * n // n_exp)
    x2d = x.reshape(n, d)

    u = _in_proj(x2d, p["norm1_g"], p["w_in"], *p["rope"], seq=seq, lru_w=lru_w, att_w=att_w)
    y_rec = _rglru(u, p["conv_w"], p["conv_b"], p["lru_wa"], p["lru_ba"], p["lru_wi"], p["lru_bi"], p["lru_L"],
                   p["lru_norm_g"], batch=batch, seq=seq, lru_w=lru_w)
    y_att = _diff_attn(u, p["diff_lambda"], p["subln_g"], batch=batch, seq=seq, lru_w=lru_w, att_w=att_w)
    x1 = _out_proj(x2d, y_rec, y_att, p["w_out"])

    logits_t = _router_logits(x1, p["norm2_g"], p["rw_t"])
    idx, gates = _route(logits_t, cap=cap)
    hid = _ffn_up(idx, x1, p["norm2_g"], p["w_gate"], p["w_up"], cap=cap)
    x2 = _ffn_down(idx, x1, hid, p["w_down"], gates, cap=cap)
    return _final_norm(x2, p["final_g"]).reshape(batch, seq, d)


def kernel(x_prompt, x_sample, norm1_g, w_in, conv_w, conv_b, lru_wa, lru_ba, lru_wi, lru_bi, lru_L, lru_norm_g,
           diff_lambda, subln_g, w_out, norm2_g, router_w, w_gate, w_up, w_down, final_g):
    assert norm1_g.shape[0] == 1, "single-layer trunk"
    row = lambda v: v.reshape(1, -1).astype(_F32)
    p = {
        "norm1_g": row(norm1_g[0]), "w_in": w_in[0].astype(_BF16),
        "conv_w": conv_w[0], "conv_b": row(conv_b[0]),
        "lru_wa": lru_wa[0].astype(_BF16), "lru_ba": lru_ba[0], "lru_wi": lru_wi[0].astype(_BF16),
        "lru_bi": lru_bi[0], "lru_L": lru_L[0], "lru_norm_g": row(lru_norm_g[0]),
        "diff_lambda": diff_lambda[0], "subln_g": row(subln_g[0]),
        "w_out": w_out[0].astype(_BF16), "norm2_g": row(norm2_g[0]),
        "rw_t": router_w[0].T.astype(_BF16),
        "w_gate": w_gate[0].astype(_BF16), "w_up": w_up[0].astype(_BF16), "w_down": w_down[0].astype(_BF16),
        "final_g": row(final_g),
        "rope": _rope_tables(x_prompt.shape[1]),
    }
    assert x_prompt.shape[1] == x_sample.shape[1], "both groups share the rotary tables"
    return _trunk(x_prompt, p), _trunk(x_sample, p)
```

```python
import functools
import math

import jax
import jax.numpy as jnp
from jax import lax
from jax.experimental import pallas as pl
from jax.experimental.pallas import tpu as pltpu

_LANES = 128
_SUBLANES = 8
_HEAD_DIM = 128
_V_DIM = 2 * _HEAD_DIM
_LRU_BLOCK = 256
_ROT_DIM = _HEAD_DIM // 4
_CONV_WIDTH = 4
_CONV_LEFT = 2
_LRU_C = 8.0
_ROPE_THETA = 500000.0
_NORM_EPS = 1e-6
_EC_CAPACITY_FACTOR = 2
_LAMBDA_INIT = 0.8 - 0.6 * math.exp(-0.3 * 0)
_V7X_VMEM_BYTES = 64 * 2**20

_F32 = jnp.float32
_BF16 = jnp.bfloat16


def _vmem_limit(estimate_bytes):
    return int(min(_V7X_VMEM_BYTES - 6 * 2**20, max(32 * 2**20, estimate_bytes + 12 * 2**20)))


def _params(semantics, estimate_bytes):
    return pltpu.CompilerParams(dimension_semantics=semantics, vmem_limit_bytes=_vmem_limit(estimate_bytes))


def _rms_scale(x, g):
    ms = jnp.mean(x * x, axis=-1, keepdims=True)
    return x * lax.rsqrt(ms + _NORM_EPS) * g


def _in_proj_kernel(x_ref, g_ref, w_ref, c_ref, sa_ref, sb_ref, o_ref, hn_ref, *, sub, q_lo, v_lo):
    j = pl.program_id(1)

    @pl.when(j == 0)
    def _():
        hn_ref[...] = _rms_scale(x_ref[...], g_ref[...]).astype(_BF16)

    def project(finish):
        hn = hn_ref[...]
        for lo in range(0, w_ref.shape[1], sub):
            acc = jnp.dot(hn, w_ref[:, lo:lo + sub], preferred_element_type=_F32)
            for grp in range(sub // _LANES):
                cols = slice(lo + grp * _LANES, lo + (grp + 1) * _LANES)
                o_ref[:, cols] = finish(acc[:, grp * _LANES:(grp + 1) * _LANES]).astype(o_ref.dtype)

    is_rope = jnp.logical_and(j >= q_lo, j < v_lo)

    @pl.when(is_rope)
    def _():
        c, sa, sb = c_ref[...], sa_ref[...], sb_ref[...]
        project(lambda blk: (blk * c + pltpu.roll(blk, _LANES - _ROT_DIM // 2, 1) * sa
                             + pltpu.roll(blk, _ROT_DIM // 2, 1) * sb))

    @pl.when(jnp.logical_not(is_rope))
    def _():
        project(lambda blk: blk)


def _in_proj(x2d, g1, w_in, rope_c, rope_sa, rope_sb, *, seq, lru_w, att_w):
    n, d = x2d.shape
    cols = w_in.shape[1]
    tm = min(512, seq)
    tn = min(1024, lru_w, att_w)
    q_lo = 2 * lru_w // tn
    k_lo = q_lo + att_w // tn
    v_lo = k_lo + att_w // tn
    sblocks = seq // tm
    table = lambda i, j: (jnp.where(j < k_lo, 0, 1), i % sblocks, 0)
    est = 2 * tm * d * 4 + tm * d * 2 + 2 * d * tn * 2 + 2 * tm * tn * 2 + 6 * tm * _LANES * 4 + 2 * tm * tn * 4
    return pl.pallas_call(
        functools.partial(_in_proj_kernel, sub=min(256, tn), q_lo=q_lo, v_lo=v_lo),
        grid=(n // tm, cols // tn),
        in_specs=[
            pl.BlockSpec((tm, d), lambda i, j: (i, 0)),
            pl.BlockSpec((1, d), lambda i, j: (0, 0)),
            pl.BlockSpec((d, tn), lambda i, j: (0, j)),
            pl.BlockSpec((None, tm, _LANES), table),
            pl.BlockSpec((None, tm, _LANES), table),
            pl.BlockSpec((None, tm, _LANES), table),
        ],
        out_specs=pl.BlockSpec((tm, tn), lambda i, j: (i, j)),
        out_shape=jax.ShapeDtypeStruct((n, cols), _BF16),
        scratch_shapes=[pltpu.VMEM((tm, d), _BF16)],
        compiler_params=_params(("arbitrary", "arbitrary"), est),
        name="in_proj",
    )(x2d, g1, w_in, rope_c, rope_sa, rope_sb)


def _softplus(z):
    return jnp.maximum(z, 0.0) + jnp.log1p(jnp.exp(-jnp.abs(z)))


def _sigmoid(z):
    return 0.5 * jnp.tanh(0.5 * z) + 0.5


def _rglru_kernel(xb_ref, gb_ref, cw_ref, cb_ref, wa_ref, ba_ref, wi_ref, bi_ref, l_ref, ng_ref, o_ref,
                  xf_ref, af_ref, bf_ref, ab_ref, bb_ref, *, seq, chunk):
    w = xb_ref.shape[1]
    pad = _SUBLANES
    nchunks = seq // chunk

    xf_ref[0:pad, :] = jnp.zeros((pad, w), _F32)
    xf_ref[seq + pad:seq + 2 * pad, :] = jnp.zeros((pad, w), _F32)

    def fill(c, carry):
        t0 = pl.multiple_of(c * chunk, chunk)
        xf_ref[pl.ds(t0 + pad, chunk), :] = xb_ref[pl.ds(t0, chunk), :].astype(_F32)
        return carry

    lax.fori_loop(0, nchunks, fill, 0)

    cw = cw_ref[...]
    cb = cb_ref[...]
    sp = _softplus(-l_ref[...])
    a_refs = (af_ref, ab_ref)
    b_refs = (bf_ref, bb_ref)

    def gates(c, carry):
        t0 = pl.multiple_of(c * chunk, chunk)
        xwin = xf_ref[pl.ds(t0, chunk + 2 * pad), :]
        xc = cb
        for k in range(_CONV_WIDTH):
            lo = pad - _CONV_LEFT + k
            xc = xc + xwin[lo:lo + chunk, :] * cw[k:k + 1, :]
        xcb = xc.astype(_BF16)
        for d in range(2):
            r = _sigmoid(jnp.dot(xcb, wa_ref[d], preferred_element_type=_F32) + ba_ref[d:d + 1, :])
            i = _sigmoid(jnp.dot(xcb, wi_ref[d], preferred_element_type=_F32) + bi_ref[d:d + 1, :])
            log_a = (-_LRU_C) * r * sp[d:d + 1, :]
            a = jnp.exp(log_a)
            a_refs[d][pl.ds(t0, chunk), :] = a
            b_refs[d][pl.ds(t0, chunk), :] = jnp.sqrt(jnp.tanh(-log_a) * (1.0 + a * a)) * (i * xc)
        return carry

    lax.fori_loop(0, nchunks, gates, 0)

    row = lax.broadcasted_iota(jnp.int32, (_SUBLANES, w), 0)
    ntiles = seq // _SUBLANES

    def scan(j, carry):
        cf, cbk = carry
        tf = pl.multiple_of(j * _SUBLANES, _SUBLANES)
        tb = pl.multiple_of((ntiles - 1 - j) * _SUBLANES, _SUBLANES)
        a = af_ref[pl.ds(tf, _SUBLANES), :]
        b = bf_ref[pl.ds(tf, _SUBLANES), :]
        a2 = ab_ref[pl.ds(tb, _SUBLANES), :]
        b2 = bb_ref[pl.ds(tb, _SUBLANES), :]
        for s in (1, 2, 4):
            keep = row >= s
            a_s = jnp.where(keep, pltpu.roll(a, s, 0), 1.0)
            b_s = jnp.where(keep, pltpu.roll(b, s, 0), 0.0)
            b = a * b_s + b
            a = a * a_s
            keep2 = row < _SUBLANES - s
            a2_s = jnp.where(keep2, pltpu.roll(a2, _SUBLANES - s, 0), 1.0)
            b2_s = jnp.where(keep2, pltpu.roll(b2, _SUBLANES - s, 0), 0.0)
            b2 = a2 * b2_s + b2
            a2 = a2 * a2_s
        h = a * cf + b
        h2 = a2 * cbk + b2
        bf_ref[pl.ds(tf, _SUBLANES), :] = h
        bb_ref[pl.ds(tb, _SUBLANES), :] = h2
        cf = jnp.broadcast_to(h[_SUBLANES - 1:_SUBLANES, :], (_SUBLANES, w))
        cbk = jnp.broadcast_to(h2[0:1, :], (_SUBLANES, w))
        return cf, cbk

    zero = jnp.zeros((_SUBLANES, w), _F32)
    lax.fori_loop(0, ntiles, scan, (zero, zero), unroll=4)

    ng = ng_ref[...]

    def finish(c, carry):
        t0 = pl.multiple_of(c * chunk, chunk)
        gate = jax.nn.gelu(gb_ref[pl.ds(t0, chunk), :].astype(_F32), approximate=True)
        y = (bf_ref[pl.ds(t0, chunk), :] + bb_ref[pl.ds(t0, chunk), :]) * gate
        o_ref[pl.ds(t0, chunk), :] = _rms_scale(y, ng).astype(o_ref.dtype)
        return carry

    lax.fori_loop(0, nchunks, finish, 0)


def _rglru(u, conv_w, conv_b, wa, ba, wi, bi, lam, norm_g, *, batch, seq, lru_w):
    nblk = lru_w // _LRU_BLOCK
    w = _LRU_BLOCK
    chunk = min(256, seq)
    est = 6 * seq * w * 2 + (seq + 16) * w * 4 + 4 * seq * w * 4 + 8 * w * w * 2 + 16 * chunk * w * 4
    blk = lambda shape, imap: pl.BlockSpec(shape, imap)
    return pl.pallas_call(
        functools.partial(_rglru_kernel, seq=seq, chunk=chunk),
        grid=(batch, nblk),
        in_specs=[
            blk((seq, w), lambda b, n: (b, n)),
            blk((seq, w), lambda b, n: (b, nblk + n)),
            blk((_CONV_WIDTH, w), lambda b, n: (0, n)),
            blk((1, w), lambda b, n: (0, n)),
            blk((2, None, w, w), lambda b, n: (0, n, 0, 0)),
            blk((2, w), lambda b, n: (0, n)),
            blk((2, None, w, w), lambda b, n: (0, n, 0, 0)),
            blk((2, w), lambda b, n: (0, n)),
            blk((2, w), lambda b, n: (0, n)),
            blk((1, w), lambda b, n: (0, n)),
        ],
        out_specs=blk((seq, w), lambda b, n: (b, n)),
        out_shape=jax.ShapeDtypeStruct((batch * seq, lru_w), _BF16),
        scratch_shapes=[pltpu.VMEM((seq + 2 * _SUBLANES, w), _F32)] + [pltpu.VMEM((seq, w), _F32)] * 4,
        compiler_params=_params(("arbitrary", "arbitrary"), est),
        name="rglru",
    )(u, u, conv_w, conv_b, wa, ba, wi, bi, lam, norm_g)


def _diff_attn_kernel(q_ref, k_ref, v_ref, lp_ref, g_ref, o_ref, kmax_ref, *, sub):
    lp = lp_ref[...]
    lam = (jnp.exp(jnp.sum(lp[0:1] * lp[1:2], axis=-1, keepdims=True))
           - jnp.exp(jnp.sum(lp[2:3] * lp[3:4], axis=-1, keepdims=True)) + _LAMBDA_INIT)
    nt = (((1,), (1,)), ((), ()))
    gain = g_ref[...] * (1.0 - _LAMBDA_INIT)
    comp = lambda ref, rows, c: ref[rows, c * _HEAD_DIM:(c + 1) * _HEAD_DIM]

    @pl.when(pl.program_id(2) == 0)
    def _():
        for c in range(2):
            kf = comp(k_ref, slice(None), c).astype(_F32)
            kk = jnp.max(jnp.sum(kf * kf, axis=-1, keepdims=True), axis=0, keepdims=True)
            kmax_ref[c:c + 1, :] = jnp.broadcast_to(jnp.sqrt(kk), (1, _LANES))

    def attend(rows, shift_of):
        parts, sums = [], []
        for c in range(2):
            qc = comp(q_ref, rows, c)
            s = lax.dot_general(qc, comp(k_ref, slice(None), c), nt, preferred_element_type=_F32)
            e = jnp.exp2(s - shift_of(c, qc, s))
            l = jnp.sum(e, axis=-1, keepdims=True)
            parts.append(jnp.dot(e.astype(_BF16), v_ref[...], preferred_element_type=_F32) * (1.0 / l))
            sums.append(l)
        o = parts[0] - lam * parts[1]
        o_ref[rows, :] = _rms_scale(o, gain).astype(o_ref.dtype)
        return jnp.minimum(sums[0], sums[1])

    def norm_bound(c, qc, s):
        qf = qc.astype(_F32)
        return jnp.sqrt(jnp.sum(qf * qf, axis=-1, keepdims=True)) * kmax_ref[c:c + 1, 0:1]

    def row_max(c, qc, s):
        return jnp.max(s, axis=-1, keepdims=True)

    groups = [slice(t * sub, (t + 1) * sub) for t in range(q_ref.shape[0] // sub)]
    low = None
    for rows in groups:
        l = attend(rows, norm_bound)
        low = l if low is None else jnp.minimum(low, l)

    safe = jnp.min(low) >= 2.0 ** -60

    @pl.when(jnp.logical_not(safe))
    def _():
        for rows in groups:
            attend(rows, row_max)


def _diff_attn(u, lam_params, subln_g, *, batch, seq, lru_w, att_w):
    heads = att_w // _V_DIM
    tq = min(1024, seq)
    qblocks = seq // tq
    q0 = 2 * lru_w // _V_DIM
    k0 = q0 + heads
    v0 = k0 + heads
    est = 2 * tq * _V_DIM * 2 + 4 * seq * _V_DIM * 2 + 2 * tq * _V_DIM * 2 + 6 * tq * seq * 4
    return pl.pallas_call(
        functools.partial(_diff_attn_kernel, sub=min(256, tq)),
        grid=(batch, heads, qblocks),
        in_specs=[
            pl.BlockSpec((tq, _V_DIM), lambda b, h, i: (b * qblocks + i, q0 + h)),
            pl.BlockSpec((seq, _V_DIM), lambda b, h, i: (b, k0 + h)),
            pl.BlockSpec((seq, _V_DIM), lambda b, h, i: (b, v0 + h)),
            pl.BlockSpec((4, _HEAD_DIM), lambda b, h, i: (0, 0)),
            pl.BlockSpec((1, _V_DIM), lambda b, h, i: (0, 0)),
        ],
        out_specs=pl.BlockSpec((tq, _V_DIM), lambda b, h, i: (b * qblocks + i, h)),
        out_shape=jax.ShapeDtypeStruct((batch * seq, att_w), _BF16),
        scratch_shapes=[pltpu.VMEM((2, _LANES), _F32)],
        compiler_params=_params(("arbitrary", "arbitrary", "arbitrary"), est),
        name="diff_attn",
    )(u, u, u, lam_params, subln_g)


def _out_proj_kernel(x_ref, a_ref, b_ref, wa_ref, wb_ref, o_ref):
    acc = jnp.dot(a_ref[...], wa_ref[...], preferred_element_type=_F32)
    acc = acc + jnp.dot(b_ref[...], wb_ref[...], preferred_element_type=_F32)
    o_ref[...] = x_ref[...] + acc


def _out_proj(x2d, y_rec, y_att, w_out):
    n, d = x2d.shape
    ka, kb = y_rec.shape[1], y_att.shape[1]
    tm = min(512, n)
    tn = min(1024, d)
    kblocks_a = ka // kb
    est = 2 * tm * tn * 4 * 2 + 2 * tm * (ka + kb) * 2 + 2 * (ka + kb) * tn * 2 + tm * tn * 4
    return pl.pallas_call(
        _out_proj_kernel,
        grid=(n // tm, d // tn),
        in_specs=[
            pl.BlockSpec((tm, tn), lambda i, j: (i, j)),
            pl.BlockSpec((tm, ka), lambda i, j: (i, 0)),
            pl.BlockSpec((tm, kb), lambda i, j: (i, 0)),
            pl.BlockSpec((ka, tn), lambda i, j: (0, j)),
            pl.BlockSpec((kb, tn), lambda i, j: (kblocks_a, j)),
        ],
        out_specs=pl.BlockSpec((tm, tn), lambda i, j: (i, j)),
        out_shape=jax.ShapeDtypeStruct((n, d), _F32),
        compiler_params=_params(("arbitrary", "arbitrary"), est),
        name="out_proj",
    )(x2d, y_rec, y_att, w_out, w_out)


def _router_kernel(x_ref, g_ref, rw_ref, o_ref):
    hn = _rms_scale(x_ref[...], g_ref[...]).astype(_BF16)
    o_ref[...] = lax.dot_general(rw_ref[...], hn, (((1,), (1,)), ((), ())), preferred_element_type=_F32)


def _router_logits(x2d, g2, rw_t):
    n, d = x2d.shape
    e = rw_t.shape[0]
    tm = min(512, n)
    est = 2 * tm * d * 4 + tm * d * 2 + 2 * e * d * 2 + 2 * e * tm * 4 + tm * d * 4
    return pl.pallas_call(
        _router_kernel,
        grid=(n // tm,),
        in_specs=[
            pl.BlockSpec((tm, d), lambda i: (i, 0)),
            pl.BlockSpec((1, d), lambda i: (0, 0)),
            pl.BlockSpec((e, d), lambda i: (0, 0)),
        ],
        out_specs=pl.BlockSpec((e, tm), lambda i: (0, i)),
        out_shape=jax.ShapeDtypeStruct((e, n), _F32),
        compiler_params=_params(("arbitrary",), est),
        name="router",
    )(x2d, g2, rw_t)


def _count(mask_f32):
    return jnp.sum(jnp.sum(mask_f32, axis=0, keepdims=True), axis=1, keepdims=True)


def _route_kernel(lg_ref, lgt_ref, idx_ref, gate_ref, aff_ref, afft_ref, *, nblocks, cap):
    n_exp, nbp, _ = lg_ref.shape

    def softmax0(l):
        ex = jnp.exp(l - jnp.max(l, axis=0, keepdims=True))
        return ex / jnp.sum(ex, axis=0, keepdims=True)

    aff_ref[...] = softmax0(lg_ref[...])
    afft_ref[...] = softmax0(lgt_ref[...])

    f32 = lambda m: jnp.where(m, 1.0, 0.0).astype(_F32)
    bf = lambda x: x.astype(_BF16)
    mm = lambda a, b: jnp.dot(a, b, preferred_element_type=_F32)

    ii = lambda shape, dim: lax.broadcasted_iota(jnp.int32, shape, dim)
    u_incl = bf(f32(ii((_LANES, _LANES), 0) <= ii((_LANES, _LANES), 1)))
    l_incl = bf(f32(ii((_LANES, _LANES), 1) <= ii((_LANES, _LANES), 0)))
    l_strict = bf(f32(ii((nbp, nbp), 1) < ii((nbp, nbp), 0)))
    u_strict = bf(f32(ii((nbp, nbp), 0) < ii((nbp, nbp), 1)))
    ones_ll = jnp.ones((_LANES, _LANES), _BF16)
    ones_8l = jnp.ones((_SUBLANES, _LANES), _BF16)
    ones_8b = jnp.ones((_SUBLANES, nbp), _BF16)
    blk_a = ii((nbp, _LANES), 0)
    blk_t = ii((_LANES, nbp), 1)
    p_row = ii((1, cap), 1).astype(_F32)
    blk_p = ii((nbp, cap), 0).astype(_F32)
    lane_p = ii((_LANES, cap), 0).astype(_F32)

    def per_expert(e, carry):
        a = aff_ref[e]
        at = afft_ref[e]
        bits = jnp.where(blk_a < nblocks, pltpu.bitcast(a, jnp.int32), -1)
        bits_t = jnp.where(blk_t < nblocks, pltpu.bitcast(at, jnp.int32), -1)

        def bit_step(i, thr):
            cand = thr | lax.shift_left(jnp.int32(1), 30 - i)
            return jnp.where(_count(f32(bits >= cand)) >= cap, cand, thr)

        thr = lax.fori_loop(0, 31, bit_step, jnp.zeros((1, 1), jnp.int32))
        gt, tie = f32(bits > thr), f32(bits == thr)
        gt_t, tie_t = f32(bits_t > thr), f32(bits_t == thr)
        need = cap - _count(gt)

        tie_b = bf(tie)
        rank = mm(l_strict, bf(mm(tie_b, ones_ll))) + mm(tie_b, u_incl) - tie
        sel = bf(jnp.where(rank < need, tie, 0.0) + gt)
        tie_tb = bf(tie_t)
        rank_t = mm(bf(mm(ones_8l, tie_tb)), u_strict)[0:1, :] + mm(l_incl, tie_tb) - tie_t
        sel_t = bf(jnp.where(rank_t < need, tie_t, 0.0) + gt_t)

        tot_b = mm(sel, ones_ll)
        cend = mm(l_strict, bf(tot_b)) + tot_b
        cl_t = bf(mm(l_incl, sel_t))
        tot_row = bf(mm(ones_8l, sel_t))

        ind_le = bf(f32(cend[:, 0:1] <= p_row))
        blk_of_p = mm(ones_8b, ind_le)[0:1, :]
        p_local = p_row - mm(tot_row, ind_le)[0:1, :]
        onehot = bf(f32(blk_p == blk_of_p))
        cl_of_p = mm(cl_t, onehot)
        off = mm(ones_8l, bf(f32(cl_of_p <= p_local)))[0:1, :]
        idx_ref[e] = (blk_of_p * _LANES + off).astype(jnp.int32)

        hi = bf(at)
        r1 = at - hi.astype(_F32)
        mid = bf(r1)
        lo = bf(r1 - mid.astype(_F32))
        g_of_p = mm(hi, onehot) + mm(mid, onehot) + mm(lo, onehot)
        gate_ref[e] = jnp.sum(jnp.where(lane_p == off, g_of_p, 0.0), axis=0, keepdims=True)
        return carry

    lax.fori_loop(0, n_exp, per_expert, 0)


def _route(logits_t, *, cap):
    e, n = logits_t.shape
    nblocks = n // _LANES
    nbp = max(nblocks, _LANES)
    lg = logits_t.reshape(e, nblocks, _LANES)
    lg = jnp.pad(lg, ((0, 0), (0, nbp - nblocks), (0, 0)))
    lgt = jnp.swapaxes(lg, 1, 2)
    est = 6 * e * nbp * _LANES * 4 + 12 * max(nbp, _LANES) * cap * 4
    full = lambda shape: pl.BlockSpec(shape, lambda: (0,) * len(shape))
    idx, gate = pl.pallas_call(
        functools.partial(_route_kernel, nblocks=nblocks, cap=cap),
        in_specs=[full((e, nbp, _LANES)), full((e, _LANES, nbp))],
        out_specs=[full((e, 1, cap)), full((e, 1, cap))],
        out_shape=[jax.ShapeDtypeStruct((e, 1, cap), jnp.int32), jax.ShapeDtypeStruct((e, 1, cap), _F32)],
        scratch_shapes=[pltpu.VMEM((e, nbp, _LANES), _F32), pltpu.VMEM((e, _LANES, nbp), _F32)],
        compiler_params=pltpu.CompilerParams(vmem_limit_bytes=_vmem_limit(est)),
        name="route",
    )(lg, lgt)
    return idx.reshape(e * cap), gate.reshape(e * cap, 1)


def _ffn_up_kernel(idx_ref, x_hbm, g_ref, wg_ref, wu_ref, o_ref, xbuf, hn_ref, sem, *, tm):
    g, f = pl.program_id(0), pl.program_id(1)
    n_tiles, n_f = pl.num_programs(0), pl.num_programs(1)
    share = tm // n_f
    norm_chunk = min(256, tm)

    def row_copy(tile, r):
        return pltpu.make_async_copy(x_hbm.at[pl.ds(idx_ref[tile * tm + r], 1)], xbuf.at[pl.ds(r, 1)], sem)

    def all_rows(tile, go):
        def body(r, carry):
            cp = row_copy(tile, r)
            cp.start() if go else cp.wait()
            return carry

        lax.fori_loop(0, tm, body, 0, unroll=8)

    @pl.when(f == 0)
    def _():
        @pl.when(g == 0)
        def _():
            all_rows(0, True)

        all_rows(g, False)

        def norm_rows(c, carry):
            r0 = pl.multiple_of(c * norm_chunk, norm_chunk)
            hn_ref[pl.ds(r0, norm_chunk), :] = _rms_scale(xbuf[pl.ds(r0, norm_chunk), :], g_ref[...]).astype(_BF16)
            return carry

        lax.fori_loop(0, tm // norm_chunk, norm_rows, 0)

    nxt = jnp.minimum(g + 1, n_tiles - 1)
    for r in range(share):
        row_copy(nxt, f * share + r).start()

    hn = hn_ref[...]
    gate = jnp.dot(hn, wg_ref[...], preferred_element_type=_F32)
    up = jnp.dot(hn, wu_ref[...], preferred_element_type=_F32)
    o_ref[...] = (jax.nn.silu(gate) * up).astype(o_ref.dtype)

    @pl.when(jnp.logical_and(g == n_tiles - 1, f == n_f - 1))
    def _():
        all_rows(nxt, False)


def _ffn_up(idx, x2d, g2, w_gate, w_up, *, cap):
    slots = idx.shape[0]
    d = x2d.shape[1]
    ff = w_gate.shape[2]
    tm = min(1024, cap)
    tf = min(512, ff)
    per_e = cap // tm
    est = tm * d * 4 + tm * d * 2 + 4 * d * tf * 2 + 2 * tm * tf * 2 + 3 * tm * tf * 4 + tm * d * 4
    return pl.pallas_call(
        functools.partial(_ffn_up_kernel, tm=tm),
        grid_spec=pltpu.PrefetchScalarGridSpec(
            num_scalar_prefetch=1,
            grid=(slots // tm, ff // tf),
            in_specs=[
                pl.BlockSpec(memory_space=pl.ANY),
                pl.BlockSpec((1, d), lambda g, f, idx_ref: (0, 0)),
                pl.BlockSpec((None, d, tf), lambda g, f, idx_ref: (g // per_e, 0, f)),
                pl.BlockSpec((None, d, tf), lambda g, f, idx_ref: (g // per_e, 0, f)),
            ],
            out_specs=pl.BlockSpec((tm, tf), lambda g, f, idx_ref: (g, f)),
            scratch_shapes=[pltpu.VMEM((tm, d), _F32), pltpu.VMEM((tm, d), _BF16), pltpu.SemaphoreType.DMA(())],
        ),
        out_shape=jax.ShapeDtypeStruct((slots, ff), _BF16),
        compiler_params=_params(("arbitrary", "arbitrary"), est),
        name="ffn_up",
    )(idx, x2d, g2, w_gate, w_up)


def _ffn_down_kernel(idx_ref, x_hbm, h_ref, w_ref, gate_ref, o_hbm, buf, sem_in, sem_out, *, rows, per_e):
    del x_hbm
    g = pl.program_id(0)
    slot = lax.rem(g, 2)
    pos = lax.rem(g, per_e)
    first = pos == 0
    last = pos == per_e - 1

    def rows_of(tile, buf_slot, fetch, go):
        base = tile * rows

        def body(r, carry):
            hbm_row = o_hbm.at[pl.ds(idx_ref[base + r], 1)]
            vmem_row = buf.at[buf_slot, pl.ds(r, 1)]
            cp = (pltpu.make_async_copy(hbm_row, vmem_row, sem_in.at[buf_slot]) if fetch
                  else pltpu.make_async_copy(vmem_row, hbm_row, sem_out.at[buf_slot]))
            cp.start() if go else cp.wait()
            return carry

        lax.fori_loop(0, rows, body, 0, unroll=8)

    @pl.when(first)
    def _():
        rows_of(g, slot, True, True)

    rows_of(g, slot, True, False)

    @pl.when(jnp.logical_not(first))
    def _():
        rows_of(g - 1, 1 - slot, False, False)

    @pl.when(jnp.logical_not(last))
    def _():
        rows_of(g + 1, 1 - slot, True, True)

    y = jnp.dot(h_ref[...], w_ref[...], preferred_element_type=_F32) * gate_ref[...]
    buf[slot] = buf[slot] + y
    rows_of(g, slot, False, True)

    @pl.when(last)
    def _():
        rows_of(g, slot, False, False)


def _ffn_down(idx, x2d, hid, w_down, gates, *, cap):
    slots, ff = hid.shape
    d = w_down.shape[2]
    rows = min(256, cap)
    assert cap % rows == 0
    per_e = cap // rows
    est = 2 * ff * d * 2 + 2 * rows * ff * 2 + 2 * rows * d * 4 + 2 * rows * d * 4 + 2 * rows * _LANES * 4
    return pl.pallas_call(
        functools.partial(_ffn_down_kernel, rows=rows, per_e=per_e),
        grid_spec=pltpu.PrefetchScalarGridSpec(
            num_scalar_prefetch=1,
            grid=(slots // rows,),
            in_specs=[pl.BlockSpec(memory_space=pl.ANY),
                      pl.BlockSpec((rows, ff), lambda g, idx_ref: (g, 0)),
                      pl.BlockSpec((None, ff, d), lambda g, idx_ref: (g // per_e, 0, 0)),
                      pl.BlockSpec((rows, 1), lambda g, idx_ref: (g, 0))],
            out_specs=pl.BlockSpec(memory_space=pl.ANY),
            scratch_shapes=[pltpu.VMEM((2, rows, d), _F32), pltpu.SemaphoreType.DMA((2,)),
                            pltpu.SemaphoreType.DMA((2,))],
        ),
        out_shape=jax.ShapeDtypeStruct(x2d.shape, x2d.dtype),
        input_output_aliases={1: 0},
        compiler_params=pltpu.CompilerParams(dimension_semantics=("arbitrary",), vmem_limit_bytes=_vmem_limit(est),
                                             has_side_effects=True),
        name="ffn_down",
    )(idx, x2d, hid, w_down, gates)


def _final_norm_kernel(x_ref, g_ref, o_ref):
    o_ref[...] = _rms_scale(x_ref[...], g_ref[...])


def _final_norm(x2d, g):
    n, d = x2d.shape
    tm = min(512, n)
    return pl.pallas_call(
        _final_norm_kernel,
        grid=(n // tm,),
        in_specs=[pl.BlockSpec((tm, d), lambda i: (i, 0)), pl.BlockSpec((1, d), lambda i: (0, 0))],
        out_specs=pl.BlockSpec((tm, d), lambda i: (i, 0)),
        out_shape=jax.ShapeDtypeStruct((n, d), _F32),
        compiler_params=_params(("arbitrary",), 5 * tm * d * 4),
        name="final_norm",
    )(x2d, g)


def _rope_tables(seq):
    half = _ROT_DIM // 2
    inv_freq = _ROPE_THETA ** (-jnp.arange(0, _ROT_DIM, 2, dtype=_F32) / _ROT_DIM)
    ang = jnp.arange(seq, dtype=_F32)[:, None] * inv_freq[None, :]
    cos, sin = jnp.cos(ang), jnp.sin(ang)
    zeros = lambda k: jnp.zeros((seq, k), _F32)
    c = jnp.concatenate([cos, cos, jnp.ones((seq, _LANES - _ROT_DIM), _F32)], axis=1)
    sa = jnp.concatenate([-sin, zeros(_LANES - half)], axis=1)
    sb = jnp.concatenate([zeros(half), sin, zeros(_LANES - _ROT_DIM)], axis=1)
    q_scale = _HEAD_DIM ** -0.5 * math.log2(math.e)
    stack = lambda t: jnp.stack([t * q_scale, t])
    return stack(c), stack(sa), stack(sb)


def _trunk(x, p):
    batch, seq, d = x.shape
    n = batch * seq
    lru_w = p["conv_w"].shape[1]
    att_w = (p["w_in"].shape[1] - 2 * lru_w) // 3
    n_exp = p["rw_t"].shape[0]
    cap = max(1, _EC_CAPACITY_FACTOR * n // n_exp)
    x2d = x.reshape(n, d)

    u = _in_proj(x2d, p["norm1_g"], p["w_in"], *p["rope"], seq=seq, lru_w=lru_w, att_w=att_w)
    y_rec = _rglru(u, p["conv_w"], p["conv_b"], p["lru_wa"], p["lru_ba"], p["lru_wi"], p["lru_bi"], p["lru_L"],
                   p["lru_norm_g"], batch=batch, seq=seq, lru_w=lru_w)
    y_att = _diff_attn(u, p["diff_lambda"], p["subln_g"], batch=batch, seq=seq, lru_w=lru_w, att_w=att_w)
    x1 = _out_proj(x2d, y_rec, y_att, p["w_out"])

    logits_t = _router_logits(x1, p["norm2_g"], p["rw_t"])
    idx, gates = _route(logits_t, cap=cap)
    hid = _ffn_up(idx, x1, p["norm2_g"], p["w_gate"], p["w_up"], cap=cap)
    x2 = _ffn_down(idx, x1, hid, p["w_down"], gates, cap=cap)
    return _final_norm(x2, p["final_g"]).reshape(batch, seq, d)


def kernel(x_prompt, x_sample, norm1_g, w_in, conv_w, conv_b, lru_wa, lru_ba, lru_wi, lru_bi, lru_L, lru_norm_g,
           diff_lambda, subln_g, w_out, norm2_g, router_w, w_gate, w_up, w_down, final_g):
    assert norm1_g.shape[0] == 1, "single-layer trunk"
    row = lambda v: v.reshape(1, -1).astype(_F32)
    p = {
        "norm1_g": row(norm1_g[0]), "w_in": w_in[0].astype(_BF16),
        "conv_w": conv_w[0], "conv_b": row(conv_b[0]),
        "lru_wa": lru_wa[0].astype(_BF16), "lru_ba": lru_ba[0], "lru_wi": lru_wi[0].astype(_BF16),
        "lru_bi": lru_bi[0], "lru_L": lru_L[0], "lru_norm_g": row(lru_norm_g[0]),
        "diff_lambda": diff_lambda[0], "subln_g": row(subln_g[0]),
        "w_out": w_out[0].astype(_BF16), "norm2_g": row(norm2_g[0]),
        "rw_t": router_w[0].T.astype(_BF16),
        "w_gate": w_gate[0].astype(_BF16), "w_up": w_up[0].astype(_BF16), "w_down": w_down[0].astype(_BF16),
        "final_g": row(final_g),
        "rope": _rope_tables(x_prompt.shape[1]),
    }
    assert x_prompt.shape[1] == x_sample.shape[1], "both groups share the rotary tables"
    return _trunk(x_prompt, p), _trunk(x_sample, p)
```

```python
import functools
import math

import jax
import jax.numpy as jnp
from jax import lax
from jax.experimental import pallas as pl
from jax.experimental.pallas import tpu as pltpu

_LANES = 128
_SUBLANES = 8
_HEAD_DIM = 128
_V_DIM = 2 * _HEAD_DIM
_LRU_BLOCK = 256
_ROT_DIM = _HEAD_DIM // 4
_CONV_WIDTH = 4
_CONV_LEFT = 2
_LRU_C = 8.0
_ROPE_THETA = 500000.0
_NORM_EPS = 1e-6
_EC_CAPACITY_FACTOR = 2
_LAMBDA_INIT = 0.8 - 0.6 * math.exp(-0.3 * 0)
_V7X_VMEM_BYTES = 64 * 2**20

_F32 = jnp.float32
_BF16 = jnp.bfloat16


def _vmem_limit(estimate_bytes):
    return int(min(_V7X_VMEM_BYTES - 6 * 2**20, max(32 * 2**20, estimate_bytes + 12 * 2**20)))


def _params(semantics, estimate_bytes):
    return pltpu.CompilerParams(dimension_semantics=semantics, vmem_limit_bytes=_vmem_limit(estimate_bytes))


def _rms_scale(x, g):
    ms = jnp.mean(x * x, axis=-1, keepdims=True)
    return x * lax.rsqrt(ms + _NORM_EPS) * g


def _in_proj_kernel(x_ref, g_ref, w_ref, c_ref, sa_ref, sb_ref, o_ref, hn_ref, *, sub, q_lo, v_lo):
    j = pl.program_id(1)

    @pl.when(j == 0)
    def _():
        hn_ref[...] = _rms_scale(x_ref[...], g_ref[...]).astype(_BF16)

    def project(finish):
        hn = hn_ref[...]
        for lo in range(0, w_ref.shape[1], sub):
            acc = jnp.dot(hn, w_ref[:, lo:lo + sub], preferred_element_type=_F32)
            for grp in range(sub // _LANES):
                cols = slice(lo + grp * _LANES, lo + (grp + 1) * _LANES)
                o_ref[:, cols] = finish(acc[:, grp * _LANES:(grp + 1) * _LANES]).astype(o_ref.dtype)

    is_rope = jnp.logical_and(j >= q_lo, j < v_lo)

    @pl.when(is_rope)
    def _():
        c, sa, sb = c_ref[...], sa_ref[...], sb_ref[...]
        project(lambda blk: (blk * c + pltpu.roll(blk, _LANES - _ROT_DIM // 2, 1) * sa
                             + pltpu.roll(blk, _ROT_DIM // 2, 1) * sb))

    @pl.when(jnp.logical_not(is_rope))
    def _():
        project(lambda blk: blk)


def _in_proj(x2d, g1, w_in, rope_c, rope_sa, rope_sb, *, seq, lru_w, att_w):
    n, d = x2d.shape
    cols = w_in.shape[1]
    tm = min(512, seq)
    tn = min(1024, lru_w, att_w)
    q_lo = 2 * lru_w // tn
    k_lo = q_lo + att_w // tn
    v_lo = k_lo + att_w // tn
    sblocks = seq // tm
    table = lambda i, j: (jnp.where(j < k_lo, 0, 1), i % sblocks, 0)
    est = 2 * tm * d * 4 + tm * d * 2 + 2 * d * tn * 2 + 2 * tm * tn * 2 + 6 * tm * _LANES * 4 + 2 * tm * tn * 4
    return pl.pallas_call(
        functools.partial(_in_proj_kernel, sub=min(256, tn), q_lo=q_lo, v_lo=v_lo),
        grid=(n // tm, cols // tn),
        in_specs=[
            pl.BlockSpec((tm, d), lambda i, j: (i, 0)),
            pl.BlockSpec((1, d), lambda i, j: (0, 0)),
            pl.BlockSpec((d, tn), lambda i, j: (0, j)),
            pl.BlockSpec((None, tm, _LANES), table),
            pl.BlockSpec((None, tm, _LANES), table),
            pl.BlockSpec((None, tm, _LANES), table),
        ],
        out_specs=pl.BlockSpec((tm, tn), lambda i, j: (i, j)),
        out_shape=jax.ShapeDtypeStruct((n, cols), _BF16),
        scratch_shapes=[pltpu.VMEM((tm, d), _BF16)],
        compiler_params=_params(("arbitrary", "arbitrary"), est),
        name="in_proj",
    )(x2d, g1, w_in, rope_c, rope_sa, rope_sb)


def _softplus(z):
    return jnp.maximum(z, 0.0) + jnp.log1p(jnp.exp(-jnp.abs(z)))


def _sigmoid(z):
    return 0.5 * jnp.tanh(0.5 * z) + 0.5


def _rglru_kernel(xb_ref, gb_ref, cw_ref, cb_ref, wa_ref, ba_ref, wi_ref, bi_ref, l_ref, ng_ref, o_ref,
                  xf_ref, af_ref, bf_ref, ab_ref, bb_ref, *, seq, chunk):
    w = xb_ref.shape[1]
    pad = _SUBLANES
    nchunks = seq // chunk

    xf_ref[0:pad, :] = jnp.zeros((pad, w), _F32)
    xf_ref[seq + pad:seq + 2 * pad, :] = jnp.zeros((pad, w), _F32)

    def fill(c, carry):
        t0 = pl.multiple_of(c * chunk, chunk)
        xf_ref[pl.ds(t0 + pad, chunk), :] = xb_ref[pl.ds(t0, chunk), :].astype(_F32)
        return carry

    lax.fori_loop(0, nchunks, fill, 0)

    cw = cw_ref[...]
    cb = cb_ref[...]
    sp = _softplus(-l_ref[...])
    a_refs = (af_ref, ab_ref)
    b_refs = (bf_ref, bb_ref)

    def gates(c, carry):
        t0 = pl.multiple_of(c * chunk, chunk)
        xwin = xf_ref[pl.ds(t0, chunk + 2 * pad), :]
        xc = cb
        for k in range(_CONV_WIDTH):
            lo = pad - _CONV_LEFT + k
            xc = xc + xwin[lo:lo + chunk, :] * cw[k:k + 1, :]
        xcb = xc.astype(_BF16)
        for d in range(2):
            r = _sigmoid(jnp.dot(xcb, wa_ref[d], preferred_element_type=_F32) + ba_ref[d:d + 1, :])
            i = _sigmoid(jnp.dot(xcb, wi_ref[d], preferred_element_type=_F32) + bi_ref[d:d + 1, :])
            log_a = (-_LRU_C) * r * sp[d:d + 1, :]
            a = jnp.exp(log_a)
            a_refs[d][pl.ds(t0, chunk), :] = a
            b_refs[d][pl.ds(t0, chunk), :] = jnp.sqrt(jnp.tanh(-log_a) * (1.0 + a * a)) * (i * xc)
        return carry

    lax.fori_loop(0, nchunks, gates, 0)

    row = lax.broadcasted_iota(jnp.int32, (_SUBLANES, w), 0)
    ntiles = seq // _SUBLANES

    def scan(j, carry):
        cf, cbk = carry
        tf = pl.multiple_of(j * _SUBLANES, _SUBLANES)
        tb = pl.multiple_of((ntiles - 1 - j) * _SUBLANES, _SUBLANES)
        a = af_ref[pl.ds(tf, _SUBLANES), :]
        b = bf_ref[pl.ds(tf, _SUBLANES), :]
        a2 = ab_ref[pl.ds(tb, _SUBLANES), :]
        b2 = bb_ref[pl.ds(tb, _SUBLANES), :]
        for s in (1, 2, 4):
            keep = row >= s
            a_s = jnp.where(keep, pltpu.roll(a, s, 0), 1.0)
            b_s = jnp.where(keep, pltpu.roll(b, s, 0), 0.0)
            b = a * b_s + b
            a = a * a_s
            keep2 = row < _SUBLANES - s
            a2_s = jnp.where(keep2, pltpu.roll(a2, _SUBLANES - s, 0), 1.0)
            b2_s = jnp.where(keep2, pltpu.roll(b2, _SUBLANES - s, 0), 0.0)
            b2 = a2 * b2_s + b2
            a2 = a2 * a2_s
        h = a * cf + b
        h2 = a2 * cbk + b2
        bf_ref[pl.ds(tf, _SUBLANES), :] = h
        bb_ref[pl.ds(tb, _SUBLANES), :] = h2
        cf = jnp.broadcast_to(h[_SUBLANES - 1:_SUBLANES, :], (_SUBLANES, w))
        cbk = jnp.broadcast_to(h2[0:1, :], (_SUBLANES, w))
        return cf, cbk

    zero = jnp.zeros((_SUBLANES, w), _F32)
    lax.fori_loop(0, ntiles, scan, (zero, zero), unroll=4)

    ng = ng_ref[...]

    def finish(c, carry):
        t0 = pl.multiple_of(c * chunk, chunk)
        gate = jax.nn.gelu(gb_ref[pl.ds(t0, chunk), :].astype(_F32), approximate=True)
        y = (bf_ref[pl.ds(t0, chunk), :] + bb_ref[pl.ds(t0, chunk), :]) * gate
        o_ref[pl.ds(t0, chunk), :] = _rms_scale(y, ng).astype(o_ref.dtype)
        return carry

    lax.fori_loop(0, nchunks, finish, 0)


def _rglru(u, conv_w, conv_b, wa, ba, wi, bi, lam, norm_g, *, batch, seq, lru_w):
    nblk = lru_w // _LRU_BLOCK
    w = _LRU_BLOCK
    chunk = min(256, seq)
    est = 6 * seq * w * 2 + (seq + 16) * w * 4 + 4 * seq * w * 4 + 8 * w * w * 2 + 16 * chunk * w * 4
    blk = lambda shape, imap: pl.BlockSpec(shape, imap)
    return pl.pallas_call(
        functools.partial(_rglru_kernel, seq=seq, chunk=chunk),
        grid=(batch, nblk),
        in_specs=[
            blk((seq, w), lambda b, n: (b, n)),
            blk((seq, w), lambda b, n: (b, nblk + n)),
            blk((_CONV_WIDTH, w), lambda b, n: (0, n)),
            blk((1, w), lambda b, n: (0, n)),
            blk((2, None, w, w), lambda b, n: (0, n, 0, 0)),
            blk((2, w), lambda b, n: (0, n)),
            blk((2, None, w, w), lambda b, n: (0, n, 0, 0)),
            blk((2, w), lambda b, n: (0, n)),
            blk((2, w), lambda b, n: (0, n)),
            blk((1, w), lambda b, n: (0, n)),
        ],
        out_specs=blk((seq, w), lambda b, n: (b, n)),
        out_shape=jax.ShapeDtypeStruct((batch * seq, lru_w), _BF16),
        scratch_shapes=[pltpu.VMEM((seq + 2 * _SUBLANES, w), _F32)] + [pltpu.VMEM((seq, w), _F32)] * 4,
        compiler_params=_params(("arbitrary", "arbitrary"), est),
        name="rglru",
    )(u, u, conv_w, conv_b, wa, ba, wi, bi, lam, norm_g)


def _diff_attn_kernel(q_ref, k_ref, v_ref, lp_ref, g_ref, o_ref, kmax_ref, *, sub):
    lp = lp_ref[...]
    lam = (jnp.exp(jnp.sum(lp[0:1] * lp[1:2], axis=-1, keepdims=True))
           - jnp.exp(jnp.sum(lp[2:3] * lp[3:4], axis=-1, keepdims=True)) + _LAMBDA_INIT)
    nt = (((1,), (1,)), ((), ()))
    gain = g_ref[...] * (1.0 - _LAMBDA_INIT)
    comp = lambda ref, rows, c: ref[rows, c * _HEAD_DIM:(c + 1) * _HEAD_DIM]

    @pl.when(pl.program_id(2) == 0)
    def _():
        for c in range(2):
            kf = comp(k_ref, slice(None), c).astype(_F32)
            kk = jnp.max(jnp.sum(kf * kf, axis=-1, keepdims=True), axis=0, keepdims=True)
            kmax_ref[c:c + 1, :] = jnp.broadcast_to(jnp.sqrt(kk), (1, _LANES))

    def attend(rows, shift_of):
        parts, sums = [], []
        for c in range(2):
            qc = comp(q_ref, rows, c)
            s = lax.dot_general(qc, comp(k_ref, slice(None), c), nt, preferred_element_type=_F32)
            e = jnp.exp2(s - shift_of(c, qc, s))
            l = jnp.sum(e, axis=-1, keepdims=True)
            parts.append(jnp.dot(e.astype(_BF16), v_ref[...], preferred_element_type=_F32) * (1.0 / l))
            sums.append(l)
        o = parts[0] - lam * parts[1]
        o_ref[rows, :] = _rms_scale(o, gain).astype(o_ref.dtype)
        return jnp.minimum(sums[0], sums[1])

    def norm_bound(c, qc, s):
        qf = qc.astype(_F32)
        return jnp.sqrt(jnp.sum(qf * qf, axis=-1, keepdims=True)) * kmax_ref[c:c + 1, 0:1]

    def row_max(c, qc, s):
        return jnp.max(s, axis=-1, keepdims=True)

    groups = [slice(t * sub, (t + 1) * sub) for t in range(q_ref.shape[0] // sub)]
    low = None
    for rows in groups:
        l = attend(rows, norm_bound)
        low = l if low is None else jnp.minimum(low, l)

    safe = jnp.min(low) >= 2.0 ** -60

    @pl.when(jnp.logical_not(safe))
    def _():
        for rows in groups:
            attend(rows, row_max)


def _diff_attn(u, lam_params, subln_g, *, batch, seq, lru_w, att_w):
    heads = att_w // _V_DIM
    tq = min(1024, seq)
    qblocks = seq // tq
    q0 = 2 * lru_w // _V_DIM
    k0 = q0 + heads
    v0 = k0 + heads
    est = 2 * tq * _V_DIM * 2 + 4 * seq * _V_DIM * 2 + 2 * tq * _V_DIM * 2 + 6 * tq * seq * 4
    return pl.pallas_call(
        functools.partial(_diff_attn_kernel, sub=min(256, tq)),
        grid=(batch, heads, qblocks),
        in_specs=[
            pl.BlockSpec((tq, _V_DIM), lambda b, h, i: (b * qblocks + i, q0 + h)),
            pl.BlockSpec((seq, _V_DIM), lambda b, h, i: (b, k0 + h)),
            pl.BlockSpec((seq, _V_DIM), lambda b, h, i: (b, v0 + h)),
            pl.BlockSpec((4, _HEAD_DIM), lambda b, h, i: (0, 0)),
            pl.BlockSpec((1, _V_DIM), lambda b, h, i: (0, 0)),
        ],
        out_specs=pl.BlockSpec((tq, _V_DIM), lambda b, h, i: (b * qblocks + i, h)),
        out_shape=jax.ShapeDtypeStruct((batch * seq, att_w), _BF16),
        scratch_shapes=[pltpu.VMEM((2, _LANES), _F32)],
        compiler_params=_params(("arbitrary", "arbitrary", "arbitrary"), est),
        name="diff_attn",
    )(u, u, u, lam_params, subln_g)


def _out_proj_kernel(x_ref, a_ref, b_ref, wa_ref, wb_ref, o_ref):
    acc = jnp.dot(a_ref[...], wa_ref[...], preferred_element_type=_F32)
    acc = acc + jnp.dot(b_ref[...], wb_ref[...], preferred_element_type=_F32)
    o_ref[...] = x_ref[...] + acc


def _out_proj(x2d, y_rec, y_att, w_out):
    n, d = x2d.shape
    ka, kb = y_rec.shape[1], y_att.shape[1]
    tm = min(512, n)
    tn = min(1024, d)
    kblocks_a = ka // kb
    est = 2 * tm * tn * 4 * 2 + 2 * tm * (ka + kb) * 2 + 2 * (ka + kb) * tn * 2 + tm * tn * 4
    return pl.pallas_call(
        _out_proj_kernel,
        grid=(n // tm, d // tn),
        in_specs=[
            pl.BlockSpec((tm, tn), lambda i, j: (i, j)),
            pl.BlockSpec((tm, ka), lambda i, j: (i, 0)),
            pl.BlockSpec((tm, kb), lambda i, j: (i, 0)),
            pl.BlockSpec((ka, tn), lambda i, j: (0, j)),
            pl.BlockSpec((kb, tn), lambda i, j: (kblocks_a, j)),
        ],
        out_specs=pl.BlockSpec((tm, tn), lambda i, j: (i, j)),
        out_shape=jax.ShapeDtypeStruct((n, d), _F32),
        compiler_params=_params(("arbitrary", "arbitrary"), est),
        name="out_proj",
    )(x2d, y_rec, y_att, w_out, w_out)


def _router_kernel(x_ref, g_ref, rw_ref, o_ref):
    hn = _rms_scale(x_ref[...], g_ref[...]).astype(_BF16)
    o_ref[...] = lax.dot_general(rw_ref[...], hn, (((1,), (1,)), ((), ())), preferred_element_type=_F32)


def _router_logits(x2d, g2, rw_t):
    n, d = x2d.shape
    e = rw_t.shape[0]
    tm = min(512, n)
    est = 2 * tm * d * 4 + tm * d * 2 + 2 * e * d * 2 + 2 * e * tm * 4 + tm * d * 4
    return pl.pallas_call(
        _router_kernel,
        grid=(n // tm,),
        in_specs=[
            pl.BlockSpec((tm, d), lambda i: (i, 0)),
            pl.BlockSpec((1, d), lambda i: (0, 0)),
            pl.BlockSpec((e, d), lambda i: (0, 0)),
        ],
        out_specs=pl.BlockSpec((e, tm), lambda i: (0, i)),
        out_shape=jax.ShapeDtypeStruct((e, n), _F32),
        compiler_params=_params(("arbitrary",), est),
        name="router",
    )(x2d, g2, rw_t)


def _count(mask_f32):
    return jnp.sum(jnp.sum(mask_f32, axis=0, keepdims=True), axis=1, keepdims=True)


def _route_kernel(lg_ref, lgt_ref, idx_ref, gate_ref, aff_ref, afft_ref, *, nblocks, cap):
    n_exp, nbp, _ = lg_ref.shape

    def softmax0(l):
        ex = jnp.exp(l - jnp.max(l, axis=0, keepdims=True))
        return ex / jnp.sum(ex, axis=0, keepdims=True)

    aff_ref[...] = softmax0(lg_ref[...])
    afft_ref[...] = softmax0(lgt_ref[...])

    f32 = lambda m: jnp.where(m, 1.0, 0.0).astype(_F32)
    bf = lambda x: x.astype(_BF16)
    mm = lambda a, b: jnp.dot(a, b, preferred_element_type=_F32)

    ii = lambda shape, dim: lax.broadcasted_iota(jnp.int32, shape, dim)
    u_incl = bf(f32(ii((_LANES, _LANES), 0) <= ii((_LANES, _LANES), 1)))
    l_incl = bf(f32(ii((_LANES, _LANES), 1) <= ii((_LANES, _LANES), 0)))
    l_strict = bf(f32(ii((nbp, nbp), 1) < ii((nbp, nbp), 0)))
    u_strict = bf(f32(ii((nbp, nbp), 0) < ii((nbp, nbp), 1)))
    ones_ll = jnp.ones((_LANES, _LANES), _BF16)
    ones_8l = jnp.ones((_SUBLANES, _LANES), _BF16)
    ones_8b = jnp.ones((_SUBLANES, nbp), _BF16)
    blk_a = ii((nbp, _LANES), 0)
    blk_t = ii((_LANES, nbp), 1)
    p_row = ii((1, cap), 1).astype(_F32)
    blk_p = ii((nbp, cap), 0).astype(_F32)
    lane_p = ii((_LANES, cap), 0).astype(_F32)

    def per_expert(e, carry):
        a = aff_ref[e]
        at = afft_ref[e]
        bits = jnp.where(blk_a < nblocks, pltpu.bitcast(a, jnp.int32), -1)
        bits_t = jnp.where(blk_t < nblocks, pltpu.bitcast(at, jnp.int32), -1)

        def bit_step(i, thr):
            cand = thr | lax.shift_left(jnp.int32(1), 30 - i)
            return jnp.where(_count(f32(bits >= cand)) >= cap, cand, thr)

        thr = lax.fori_loop(0, 31, bit_step, jnp.zeros((1, 1), jnp.int32))
        gt, tie = f32(bits > thr), f32(bits == thr)
        gt_t, tie_t = f32(bits_t > thr), f32(bits_t == thr)
        need = cap - _count(gt)

        tie_b = bf(tie)
        rank = mm(l_strict, bf(mm(tie_b, ones_ll))) + mm(tie_b, u_incl) - tie
        sel = bf(jnp.where(rank < need, tie, 0.0) + gt)
        tie_tb = bf(tie_t)
        rank_t = mm(bf(mm(ones_8l, tie_tb)), u_strict)[0:1, :] + mm(l_incl, tie_tb) - tie_t
        sel_t = bf(jnp.where(rank_t < need, tie_t, 0.0) + gt_t)

        tot_b = mm(sel, ones_ll)
        cend = mm(l_strict, bf(tot_b)) + tot_b
        cl_t = bf(mm(l_incl, sel_t))
        tot_row = bf(mm(ones_8l, sel_t))

        ind_le = bf(f32(cend[:, 0:1] <= p_row))
        blk_of_p = mm(ones_8b, ind_le)[0:1, :]
        p_local = p_row - mm(tot_row, ind_le)[0:1, :]
        onehot = bf(f32(blk_p == blk_of_p))
        cl_of_p = mm(cl_t, onehot)
        off = mm(ones_8l, bf(f32(cl_of_p <= p_local)))[0:1, :]
        idx_ref[e] = (blk_of_p * _LANES + off).astype(jnp.int32)

        hi = bf(at)
        r1 = at - hi.astype(_F32)
        mid = bf(r1)
        lo = bf(r1 - mid.astype(_F32))
        g_of_p = mm(hi, onehot) + mm(mid, onehot) + mm(lo, onehot)
        gate_ref[e] = jnp.sum(jnp.where(lane_p == off, g_of_p, 0.0), axis=0, keepdims=True)
        return carry

    lax.fori_loop(0, n_exp, per_expert, 0)


def _route(logits_t, *, cap):
    e, n = logits_t.shape
    nblocks = n // _LANES
    nbp = max(nblocks, _LANES)
    lg = logits_t.reshape(e, nblocks, _LANES)
    lg = jnp.pad(lg, ((0, 0), (0, nbp - nblocks), (0, 0)))
    lgt = jnp.swapaxes(lg, 1, 2)
    est = 6 * e * nbp * _LANES * 4 + 12 * max(nbp, _LANES) * cap * 4
    full = lambda shape: pl.BlockSpec(shape, lambda: (0,) * len(shape))
    idx, gate = pl.pallas_call(
        functools.partial(_route_kernel, nblocks=nblocks, cap=cap),
        in_specs=[full((e, nbp, _LANES)), full((e, _LANES, nbp))],
        out_specs=[full((e, 1, cap)), full((e, 1, cap))],
        out_shape=[jax.ShapeDtypeStruct((e, 1, cap), jnp.int32), jax.ShapeDtypeStruct((e, 1, cap), _F32)],
        scratch_shapes=[pltpu.VMEM((e, nbp, _LANES), _F32), pltpu.VMEM((e, _LANES, nbp), _F32)],
        compiler_params=pltpu.CompilerParams(vmem_limit_bytes=_vmem_limit(est)),
        name="route",
    )(lg, lgt)
    return idx.reshape(e * cap), gate.reshape(e * cap, 1)


def _ffn_up_kernel(idx_ref, x_hbm, g_ref, wg_ref, wu_ref, o_ref, xbuf, hn_ref, sem, *, tm):
    g, f = pl.program_id(0), pl.program_id(1)
    n_tiles, n_f = pl.num_programs(0), pl.num_programs(1)
    share = tm // n_f
    norm_chunk = min(256, tm)

    def row_copy(tile, r):
        return pltpu.make_async_copy(x_hbm.at[pl.ds(idx_ref[tile * tm + r], 1)], xbuf.at[pl.ds(r, 1)], sem)

    def all_rows(tile, go):
        def body(r, carry):
            cp = row_copy(tile, r)
            cp.start() if go else cp.wait()
            return carry

        lax.fori_loop(0, tm, body, 0, unroll=8)

    @pl.when(f == 0)
    def _():
        @pl.when(g == 0)
        def _():
            all_rows(0, True)

        all_rows(g, False)

        def norm_rows(c, carry):
            r0 = pl.multiple_of(c * norm_chunk, norm_chunk)
            hn_ref[pl.ds(r0, norm_chunk), :] = _rms_scale(xbuf[pl.ds(r0, norm_chunk), :], g_ref[...]).astype(_BF16)
            return carry

        lax.fori_loop(0, tm // norm_chunk, norm_rows, 0)

    nxt = jnp.minimum(g + 1, n_tiles - 1)
    for r in range(share):
        row_copy(nxt, f * share + r).start()

    hn = hn_ref[...]
    gate = jnp.dot(hn, wg_ref[...].astype(_BF16), preferred_element_type=_F32)
    up = jnp.dot(hn, wu_ref[...].astype(_BF16), preferred_element_type=_F32)
    o_ref[...] = (jax.nn.silu(gate) * up).astype(o_ref.dtype)

    @pl.when(jnp.logical_and(g == n_tiles - 1, f == n_f - 1))
    def _():
        all_rows(nxt, False)


def _ffn_up(idx, x2d, g2, w_gate, w_up, *, cap):
    slots = idx.shape[0]
    d = x2d.shape[1]
    ff = w_gate.shape[2]
    tm = min(1024, cap)
    tf = min(256, ff)
    per_e = cap // tm
    est = tm * d * 4 + tm * d * 2 + 4 * d * tf * 4 + 2 * d * tf * 2 + 2 * tm * tf * 2 + 3 * tm * tf * 4 + tm * d * 4
    return pl.pallas_call(
        functools.partial(_ffn_up_kernel, tm=tm),
        grid_spec=pltpu.PrefetchScalarGridSpec(
            num_scalar_prefetch=1,
            grid=(slots // tm, ff // tf),
            in_specs=[
                pl.BlockSpec(memory_space=pl.ANY),
                pl.BlockSpec((1, d), lambda g, f, idx_ref: (0, 0)),
                pl.BlockSpec((None, d, tf), lambda g, f, idx_ref: (g // per_e, 0, f)),
                pl.BlockSpec((None, d, tf), lambda g, f, idx_ref: (g // per_e, 0, f)),
            ],
            out_specs=pl.BlockSpec((tm, tf), lambda g, f, idx_ref: (g, f)),
            scratch_shapes=[pltpu.VMEM((tm, d), _F32), pltpu.VMEM((tm, d), _BF16), pltpu.SemaphoreType.DMA(())],
        ),
        out_shape=jax.ShapeDtypeStruct((slots, ff), _BF16),
        compiler_params=_params(("arbitrary", "arbitrary"), est),
        name="ffn_up",
    )(idx, x2d, g2, w_gate, w_up)


def _ffn_down_kernel(idx_ref, x_hbm, h_ref, w_ref, gate_ref, o_hbm, buf, sem_in, sem_out, *, rows, per_e):
    del x_hbm
    g = pl.program_id(0)
    slot = lax.rem(g, 2)
    pos = lax.rem(g, per_e)
    first = pos == 0
    last = pos == per_e - 1

    def rows_of(tile, buf_slot, fetch, go):
        base = tile * rows

        for r in range(rows):
            hbm_row = o_hbm.at[pl.ds(idx_ref[base + r], 1)]
            vmem_row = buf.at[buf_slot, pl.ds(r, 1)]
            cp = (pltpu.make_async_copy(hbm_row, vmem_row, sem_in.at[buf_slot]) if fetch
                  else pltpu.make_async_copy(vmem_row, hbm_row, sem_out.at[buf_slot]))
            cp.start() if go else cp.wait()

    @pl.when(first)
    def _():
        rows_of(g, slot, True, True)

    rows_of(g, slot, True, False)

    @pl.when(jnp.logical_not(first))
    def _():
        rows_of(g - 1, 1 - slot, False, False)

    @pl.when(jnp.logical_not(last))
    def _():
        rows_of(g + 1, 1 - slot, True, True)

    y = jnp.dot(h_ref[...], w_ref[...], preferred_element_type=_F32) * gate_ref[...]
    buf[slot] = buf[slot] + y
    rows_of(g, slot, False, True)

    @pl.when(last)
    def _():
        rows_of(g, slot, False, False)


def _ffn_down(idx, x2d, hid, w_down, gates, *, cap):
    slots, ff = hid.shape
    d = w_down.shape[2]
    rows = min(256, cap)
    assert cap % rows == 0
    per_e = cap // rows
    est = 2 * ff * d * 2 + 2 * rows * ff * 2 + 2 * rows * d * 4 + 2 * rows * d * 4 + 2 * rows * _LANES * 4
    return pl.pallas_call(
        functools.partial(_ffn_down_kernel, rows=rows, per_e=per_e),
        grid_spec=pltpu.PrefetchScalarGridSpec(
            num_scalar_prefetch=1,
            grid=(slots // rows,),
            in_specs=[pl.BlockSpec(memory_space=pl.ANY),
                      pl.BlockSpec((rows, ff), lambda g, idx_ref: (g, 0)),
                      pl.BlockSpec((None, ff, d), lambda g, idx_ref: (g // per_e, 0, 0)),
                      pl.BlockSpec((rows, 1), lambda g, idx_ref: (g, 0))],
            out_specs=pl.BlockSpec(memory_space=pl.ANY),
            scratch_shapes=[pltpu.VMEM((2, rows, d), _F32), pltpu.SemaphoreType.DMA((2,)),
                            pltpu.SemaphoreType.DMA((2,))],
        ),
        out_shape=jax.ShapeDtypeStruct(x2d.shape, x2d.dtype),
        input_output_aliases={1: 0},
        compiler_params=pltpu.CompilerParams(dimension_semantics=("arbitrary",), vmem_limit_bytes=_vmem_limit(est),
                                             has_side_effects=True),
        name="ffn_down",
    )(idx, x2d, hid, w_down, gates)


def _final_norm_kernel(x_ref, g_ref, o_ref):
    o_ref[...] = _rms_scale(x_ref[...], g_ref[...])


def _final_norm(x2d, g):
    n, d = x2d.shape
    tm = min(512, n)
    return pl.pallas_call(
        _final_norm_kernel,
        grid=(n // tm,),
        in_specs=[pl.BlockSpec((tm, d), lambda i: (i, 0)), pl.BlockSpec((1, d), lambda i: (0, 0))],
        out_specs=pl.BlockSpec((tm, d), lambda i: (i, 0)),
        out_shape=jax.ShapeDtypeStruct((n, d), _F32),
        compiler_params=_params(("arbitrary",), 5 * tm * d * 4),
        name="final_norm",
    )(x2d, g)


def _rope_tables(seq):
    half = _ROT_DIM // 2
    inv_freq = _ROPE_THETA ** (-jnp.arange(0, _ROT_DIM, 2, dtype=_F32) / _ROT_DIM)
    ang = jnp.arange(seq, dtype=_F32)[:, None] * inv_freq[None, :]
    cos, sin = jnp.cos(ang), jnp.sin(ang)
    zeros = lambda k: jnp.zeros((seq, k), _F32)
    c = jnp.concatenate([cos, cos, jnp.ones((seq, _LANES - _ROT_DIM), _F32)], axis=1)
    sa = jnp.concatenate([-sin, zeros(_LANES - half)], axis=1)
    sb = jnp.concatenate([zeros(half), sin, zeros(_LANES - _ROT_DIM)], axis=1)
    q_scale = _HEAD_DIM ** -0.5 * math.log2(math.e)
    stack = lambda t: jnp.stack([t * q_scale, t])
    return stack(c), stack(sa), stack(sb)


def _trunk(x, p):
    batch, seq, d = x.shape
    n = batch * seq
    lru_w = p["conv_w"].shape[1]
    att_w = (p["w_in"].shape[1] - 2 * lru_w) // 3
    n_exp = p["rw_t"].shape[0]
    cap = max(1, _EC_CAPACITY_FACTOR * n // n_exp)
    x2d = x.reshape(n, d)

    u = _in_proj(x2d, p["norm1_g"], p["w_in"], *p["rope"], seq=seq, lru_w=lru_w, att_w=att_w)
    y_rec = _rglru(u, p["conv_w"], p["conv_b"], p["lru_wa"], p["lru_ba"], p["lru_wi"], p["lru_bi"], p["lru_L"],
                   p["lru_norm_g"], batch=batch, seq=seq, lru_w=lru_w)
    y_att = _diff_attn(u, p["diff_lambda"], p["subln_g"], batch=batch, seq=seq, lru_w=lru_w, att_w=att_w)
    x1 = _out_proj(x2d, y_rec, y_att, p["w_out"])

    logits_t = _router_logits(x1, p["norm2_g"], p["rw_t"])
    idx, gates = _route(logits_t, cap=cap)
    hid = _ffn_up(idx, x1, p["norm2_g"], p["w_gate"], p["w_up"], cap=cap)
    x2 = _ffn_down(idx, x1, hid, p["w_down"], gates, cap=cap)
    return _final_norm(x2, p["final_g"]).reshape(batch, seq, d)


def kernel(x_prompt, x_sample, norm1_g, w_in, conv_w, conv_b, lru_wa, lru_ba, lru_wi, lru_bi, lru_L, lru_norm_g,
           diff_lambda, subln_g, w_out, norm2_g, router_w, w_gate, w_up, w_down, final_g):
    assert norm1_g.shape[0] == 1, "single-layer trunk"
    row = lambda v: v.reshape(1, -1).astype(_F32)
    p = {
        "norm1_g": row(norm1_g[0]), "w_in": w_in[0].astype(_BF16),
        "conv_w": conv_w[0], "conv_b": row(conv_b[0]),
        "lru_wa": lru_wa[0].astype(_BF16), "lru_ba": lru_ba[0], "lru_wi": lru_wi[0].astype(_BF16),
        "lru_bi": lru_bi[0], "lru_L": lru_L[0], "lru_norm_g": row(lru_norm_g[0]),
        "diff_lambda": diff_lambda[0], "subln_g": row(subln_g[0]),
        "w_out": w_out[0].astype(_BF16), "norm2_g": row(norm2_g[0]),
        "rw_t": router_w[0].T.astype(_BF16),
        "w_gate": w_gate[0], "w_up": w_up[0], "w_down": w_down[0].astype(_BF16),
        "final_g": row(final_g),
        "rope": _rope_tables(x_prompt.shape[1]),
    }
    assert x_prompt.shape[1] == x_sample.shape[1], "both groups share the rotary tables"
    return _trunk(x_prompt, p), _trunk(x_sample, p)
```

```python
import functools
import math

import jax
import jax.numpy as jnp
from jax import lax
from jax.experimental import pallas as pl
from jax.experimental.pallas import tpu as pltpu

_LANES = 128
_SUBLANES = 8
_HEAD_DIM = 128
_V_DIM = 2 * _HEAD_DIM
_LRU_BLOCK = 256
_ROT_DIM = _HEAD_DIM // 4
_CONV_WIDTH = 4
_CONV_LEFT = 2
_LRU_C = 8.0
_ROPE_THETA = 500000.0
_NORM_EPS = 1e-6
_EC_CAPACITY_FACTOR = 2
_LAMBDA_INIT = 0.8 - 0.6 * math.exp(-0.3 * 0)
_V7X_VMEM_BYTES = 64 * 2**20

_F32 = jnp.float32
_BF16 = jnp.bfloat16


def _vmem_limit(estimate_bytes):
    return int(min(_V7X_VMEM_BYTES - 6 * 2**20, max(32 * 2**20, estimate_bytes + 12 * 2**20)))


def _params(semantics, estimate_bytes):
    return pltpu.CompilerParams(dimension_semantics=semantics, vmem_limit_bytes=_vmem_limit(estimate_bytes))


def _rms_scale(x, g):
    ms = jnp.mean(x * x, axis=-1, keepdims=True)
    return x * lax.rsqrt(ms + _NORM_EPS) * g


def _in_proj_kernel(x_ref, g_ref, w_ref, c_ref, sa_ref, sb_ref, o_ref, hn_ref, *, sub, q_lo, v_lo):
    j = pl.program_id(1)

    @pl.when(j == 0)
    def _():
        hn_ref[...] = _rms_scale(x_ref[...], g_ref[...]).astype(_BF16)

    def project(finish):
        hn = hn_ref[...]
        for lo in range(0, w_ref.shape[1], sub):
            acc = jnp.dot(hn, w_ref[:, lo:lo + sub], preferred_element_type=_F32)
            for grp in range(sub // _LANES):
                cols = slice(lo + grp * _LANES, lo + (grp + 1) * _LANES)
                o_ref[:, cols] = finish(acc[:, grp * _LANES:(grp + 1) * _LANES]).astype(o_ref.dtype)

    is_rope = jnp.logical_and(j >= q_lo, j < v_lo)

    @pl.when(is_rope)
    def _():
        c, sa, sb = c_ref[...], sa_ref[...], sb_ref[...]
        project(lambda blk: (blk * c + pltpu.roll(blk, _LANES - _ROT_DIM // 2, 1) * sa
                             + pltpu.roll(blk, _ROT_DIM // 2, 1) * sb))

    @pl.when(jnp.logical_not(is_rope))
    def _():
        project(lambda blk: blk)


def _in_proj(x2d, g1, w_in, rope_c, rope_sa, rope_sb, *, seq, lru_w, att_w):
    n, d = x2d.shape
    cols = w_in.shape[1]
    tm = min(512, seq)
    tn = min(1024, lru_w, att_w)
    q_lo = 2 * lru_w // tn
    k_lo = q_lo + att_w // tn
    v_lo = k_lo + att_w // tn
    sblocks = seq // tm
    table = lambda i, j: (jnp.where(j < k_lo, 0, 1), i % sblocks, 0)
    est = 2 * tm * d * 4 + tm * d * 2 + 2 * d * tn * 2 + 2 * tm * tn * 2 + 6 * tm * _LANES * 4 + 2 * tm * tn * 4
    return pl.pallas_call(
        functools.partial(_in_proj_kernel, sub=min(256, tn), q_lo=q_lo, v_lo=v_lo),
        grid=(n // tm, cols // tn),
        in_specs=[
            pl.BlockSpec((tm, d), lambda i, j: (i, 0)),
            pl.BlockSpec((1, d), lambda i, j: (0, 0)),
            pl.BlockSpec((d, tn), lambda i, j: (0, j)),
            pl.BlockSpec((None, tm, _LANES), table),
            pl.BlockSpec((None, tm, _LANES), table),
            pl.BlockSpec((None, tm, _LANES), table),
        ],
        out_specs=pl.BlockSpec((tm, tn), lambda i, j: (i, j)),
        out_shape=jax.ShapeDtypeStruct((n, cols), _BF16),
        scratch_shapes=[pltpu.VMEM((tm, d), _BF16)],
        compiler_params=_params(("arbitrary", "arbitrary"), est),
        name="in_proj",
    )(x2d, g1, w_in, rope_c, rope_sa, rope_sb)


def _softplus(z):
    return jnp.maximum(z, 0.0) + jnp.log1p(jnp.exp(-jnp.abs(z)))


def _rglru_kernel(xb_ref, gb_ref, cw_ref, cb_ref, wa_ref, ba_ref, wi_ref, bi_ref, l_ref, ng_ref, o_ref,
                  xf_ref, af_ref, bf_ref, ab_ref, bb_ref, *, seq, chunk):
    w = xb_ref.shape[1]
    pad = _SUBLANES
    nchunks = seq // chunk

    xf_ref[0:pad, :] = jnp.zeros((pad, w), _F32)
    xf_ref[seq + pad:seq + 2 * pad, :] = jnp.zeros((pad, w), _F32)

    def fill(c, carry):
        t0 = pl.multiple_of(c * chunk, chunk)
        xf_ref[pl.ds(t0 + pad, chunk), :] = xb_ref[pl.ds(t0, chunk), :].astype(_F32)
        return carry

    lax.fori_loop(0, nchunks, fill, 0)

    cw = cw_ref[...]
    cb = cb_ref[...]
    half_decay = (0.5 * _LRU_C) * _softplus(-l_ref[...])
    half_decay_log2 = (-math.log2(math.e)) * half_decay
    half_ba = 0.5 * ba_ref[...]
    half_bi = 0.5 * bi_ref[...]
    a_refs = (af_ref, ab_ref)
    b_refs = (bf_ref, bb_ref)

    def gates(c, carry):
        t0 = pl.multiple_of(c * chunk, chunk)
        xwin = xf_ref[pl.ds(t0, chunk + 2 * pad), :]
        xc = cb
        for k in range(_CONV_WIDTH):
            lo = pad - _CONV_LEFT + k
            xc = xc + xwin[lo:lo + chunk, :] * cw[k:k + 1, :]
        xcb = xc.astype(_BF16)
        half_xc = 0.5 * xc
        for d in range(2):
            row = slice(d, d + 1)
            t_r = jnp.tanh(jnp.dot(xcb, wa_ref[d], preferred_element_type=_F32) + half_ba[row, :])
            t_i = jnp.tanh(jnp.dot(xcb, wi_ref[d], preferred_element_type=_F32) + half_bi[row, :])
            neg_log_a = t_r * half_decay[row, :] + half_decay[row, :]
            a = jnp.exp2(t_r * half_decay_log2[row, :] + half_decay_log2[row, :])
            a_refs[d][pl.ds(t0, chunk), :] = a
            gated_x = t_i * half_xc + half_xc
            b_refs[d][pl.ds(t0, chunk), :] = jnp.sqrt(jnp.tanh(neg_log_a) * (1.0 + a * a)) * gated_x
        return carry

    lax.fori_loop(0, nchunks, gates, 0)

    row = lax.broadcasted_iota(jnp.int32, (_SUBLANES, w), 0)
    ntiles = seq // _SUBLANES

    def scan(j, carry):
        cf, cbk = carry
        tf = pl.multiple_of(j * _SUBLANES, _SUBLANES)
        tb = pl.multiple_of((ntiles - 1 - j) * _SUBLANES, _SUBLANES)
        a = af_ref[pl.ds(tf, _SUBLANES), :]
        b = bf_ref[pl.ds(tf, _SUBLANES), :]
        a2 = ab_ref[pl.ds(tb, _SUBLANES), :]
        b2 = bb_ref[pl.ds(tb, _SUBLANES), :]
        for s in (1, 2, 4):
            keep = row >= s
            a_s = jnp.where(keep, pltpu.roll(a, s, 0), 1.0)
            b_s = jnp.where(keep, pltpu.roll(b, s, 0), 0.0)
            b = a * b_s + b
            a = a * a_s
            keep2 = row < _SUBLANES - s
            a2_s = jnp.where(keep2, pltpu.roll(a2, _SUBLANES - s, 0), 1.0)
            b2_s = jnp.where(keep2, pltpu.roll(b2, _SUBLANES - s, 0), 0.0)
            b2 = a2 * b2_s + b2
            a2 = a2 * a2_s
        h = a * cf + b
        h2 = a2 * cbk + b2
        bf_ref[pl.ds(tf, _SUBLANES), :] = h
        bb_ref[pl.ds(tb, _SUBLANES), :] = h2
        cf = jnp.broadcast_to(h[_SUBLANES - 1:_SUBLANES, :], (_SUBLANES, w))
        cbk = jnp.broadcast_to(h2[0:1, :], (_SUBLANES, w))
        return cf, cbk

    zero = jnp.zeros((_SUBLANES, w), _F32)
    lax.fori_loop(0, ntiles, scan, (zero, zero), unroll=4)

    ng = ng_ref[...]

    def finish(c, carry):
        t0 = pl.multiple_of(c * chunk, chunk)
        gate = jax.nn.gelu(gb_ref[pl.ds(t0, chunk), :].astype(_F32), approximate=True)
        y = (bf_ref[pl.ds(t0, chunk), :] + bb_ref[pl.ds(t0, chunk), :]) * gate
        o_ref[pl.ds(t0, chunk), :] = _rms_scale(y, ng).astype(o_ref.dtype)
        return carry

    lax.fori_loop(0, nchunks, finish, 0)


def _rglru(u, conv_w, conv_b, wa, ba, wi, bi, lam, norm_g, *, batch, seq, lru_w):
    nblk = lru_w // _LRU_BLOCK
    w = _LRU_BLOCK
    chunk = min(256, seq)
    est = 6 * seq * w * 2 + (seq + 16) * w * 4 + 4 * seq * w * 4 + 8 * w * w * 2 + 16 * chunk * w * 4
    blk = lambda shape, imap: pl.BlockSpec(shape, imap)
    return pl.pallas_call(
        functools.partial(_rglru_kernel, seq=seq, chunk=chunk),
        grid=(batch, nblk),
        in_specs=[
            blk((seq, w), lambda b, n: (b, n)),
            blk((seq, w), lambda b, n: (b, nblk + n)),
            blk((_CONV_WIDTH, w), lambda b, n: (0, n)),
            blk((1, w), lambda b, n: (0, n)),
            blk((2, None, w, w), lambda b, n: (0, n, 0, 0)),
            blk((2, w), lambda b, n: (0, n)),
            blk((2, None, w, w), lambda b, n: (0, n, 0, 0)),
            blk((2, w), lambda b, n: (0, n)),
            blk((2, w), lambda b, n: (0, n)),
            blk((1, w), lambda b, n: (0, n)),
        ],
        out_specs=blk((seq, w), lambda b, n: (b, n)),
        out_shape=jax.ShapeDtypeStruct((batch * seq, lru_w), _BF16),
        scratch_shapes=[pltpu.VMEM((seq + 2 * _SUBLANES, w), _F32)] + [pltpu.VMEM((seq, w), _F32)] * 4,
        compiler_params=_params(("arbitrary", "arbitrary"), est),
        name="rglru",
    )(u, u, conv_w, conv_b, wa, ba, wi, bi, lam, norm_g)


def _diff_attn_kernel(q_ref, k_ref, v_ref, lp_ref, g_ref, o_ref, kmax_ref, *, sub):
    lp = lp_ref[...]
    lam = (jnp.exp(jnp.sum(lp[0:1] * lp[1:2], axis=-1, keepdims=True))
           - jnp.exp(jnp.sum(lp[2:3] * lp[3:4], axis=-1, keepdims=True)) + _LAMBDA_INIT)
    nt = (((1,), (1,)), ((), ()))
    gain = g_ref[...] * (1.0 - _LAMBDA_INIT)
    comp = lambda ref, rows, c: ref[rows, c * _HEAD_DIM:(c + 1) * _HEAD_DIM]

    @pl.when(pl.program_id(2) == 0)
    def _():
        for c in range(2):
            kf = comp(k_ref, slice(None), c).astype(_F32)
            kk = jnp.max(jnp.sum(kf * kf, axis=-1, keepdims=True), axis=0, keepdims=True)
            kmax_ref[c:c + 1, :] = jnp.broadcast_to(jnp.sqrt(kk), (1, _LANES))

    def attend(rows, shift_of):
        parts, sums = [], []
        for c in range(2):
            qc = comp(q_ref, rows, c)
            s = lax.dot_general(qc, comp(k_ref, slice(None), c), nt, preferred_element_type=_F32)
            e = jnp.exp2(s - shift_of(c, qc, s))
            l = jnp.sum(e, axis=-1, keepdims=True)
            parts.append(jnp.dot(e.astype(_BF16), v_ref[...], preferred_element_type=_F32) * (1.0 / l))
            sums.append(l)
        o = parts[0] - lam * parts[1]
        o_ref[rows, :] = _rms_scale(o, gain).astype(o_ref.dtype)
        return jnp.minimum(sums[0], sums[1])

    def norm_bound(c, qc, s):
        qf = qc.astype(_F32)
        return jnp.sqrt(jnp.sum(qf * qf, axis=-1, keepdims=True)) * kmax_ref[c:c + 1, 0:1]

    def row_max(c, qc, s):
        return jnp.max(s, axis=-1, keepdims=True)

    groups = [slice(t * sub, (t + 1) * sub) for t in range(q_ref.shape[0] // sub)]
    low = None
    for rows in groups:
        l = attend(rows, norm_bound)
        low = l if low is None else jnp.minimum(low, l)

    safe = jnp.min(low) >= 2.0 ** -60

    @pl.when(jnp.logical_not(safe))
    def _():
        for rows in groups:
            attend(rows, row_max)


def _diff_attn(u, lam_params, subln_g, *, batch, seq, lru_w, att_w):
    heads = att_w // _V_DIM
    tq = min(1024, seq)
    qblocks = seq // tq
    q0 = 2 * lru_w // _V_DIM
    k0 = q0 + heads
    v0 = k0 + heads
    est = 2 * tq * _V_DIM * 2 + 4 * seq * _V_DIM * 2 + 2 * tq * _V_DIM * 2 + 6 * tq * seq * 4
    return pl.pallas_call(
        functools.partial(_diff_attn_kernel, sub=min(256, tq)),
        grid=(batch, heads, qblocks),
        in_specs=[
            pl.BlockSpec((tq, _V_DIM), lambda b, h, i: (b * qblocks + i, q0 + h)),
            pl.BlockSpec((seq, _V_DIM), lambda b, h, i: (b, k0 + h)),
            pl.BlockSpec((seq, _V_DIM), lambda b, h, i: (b, v0 + h)),
            pl.BlockSpec((4, _HEAD_DIM), lambda b, h, i: (0, 0)),
            pl.BlockSpec((1, _V_DIM), lambda b, h, i: (0, 0)),
        ],
        out_specs=pl.BlockSpec((tq, _V_DIM), lambda b, h, i: (b * qblocks + i, h)),
        out_shape=jax.ShapeDtypeStruct((batch * seq, att_w), _BF16),
        scratch_shapes=[pltpu.VMEM((2, _LANES), _F32)],
        compiler_params=_params(("arbitrary", "arbitrary", "arbitrary"), est),
        name="diff_attn",
    )(u, u, u, lam_params, subln_g)


def _out_proj_kernel(x_ref, a_ref, b_ref, wa_ref, wb_ref, o_ref):
    acc = jnp.dot(a_ref[...], wa_ref[...], preferred_element_type=_F32)
    acc = acc + jnp.dot(b_ref[...], wb_ref[...], preferred_element_type=_F32)
    o_ref[...] = x_ref[...] + acc


def _out_proj(x2d, y_rec, y_att, w_out):
    n, d = x2d.shape
    ka, kb = y_rec.shape[1], y_att.shape[1]
    tm = min(1024, n)
    tn = min(1024, d)
    kblocks_a = ka // kb
    est = 2 * tm * tn * 4 * 2 + 2 * tm * (ka + kb) * 2 + 2 * (ka + kb) * tn * 2 + tm * tn * 4
    return pl.pallas_call(
        _out_proj_kernel,
        grid=(n // tm, d // tn),
        in_specs=[
            pl.BlockSpec((tm, tn), lambda i, j: (i, j)),
            pl.BlockSpec((tm, ka), lambda i, j: (i, 0)),
            pl.BlockSpec((tm, kb), lambda i, j: (i, 0)),
            pl.BlockSpec((ka, tn), lambda i, j: (0, j)),
            pl.BlockSpec((kb, tn), lambda i, j: (kblocks_a, j)),
        ],
        out_specs=pl.BlockSpec((tm, tn), lambda i, j: (i, j)),
        out_shape=jax.ShapeDtypeStruct((n, d), _F32),
        compiler_params=_params(("arbitrary", "arbitrary"), est),
        name="out_proj",
    )(x2d, y_rec, y_att, w_out, w_out)


def _router_kernel(x_ref, g_ref, rw_ref, o_ref):
    hn = _rms_scale(x_ref[...], g_ref[...]).astype(_BF16)
    o_ref[...] = lax.dot_general(rw_ref[...], hn, (((1,), (1,)), ((), ())), preferred_element_type=_F32)


def _router_logits(x2d, g2, rw_t):
    n, d = x2d.shape
    e = rw_t.shape[0]
    tm = min(512, n)
    est = 2 * tm * d * 4 + tm * d * 2 + 2 * e * d * 2 + 2 * e * tm * 4 + tm * d * 4
    return pl.pallas_call(
        _router_kernel,
        grid=(n // tm,),
        in_specs=[
            pl.BlockSpec((tm, d), lambda i: (i, 0)),
            pl.BlockSpec((1, d), lambda i: (0, 0)),
            pl.BlockSpec((e, d), lambda i: (0, 0)),
        ],
        out_specs=pl.BlockSpec((e, tm), lambda i: (0, i)),
        out_shape=jax.ShapeDtypeStruct((e, n), _F32),
        compiler_params=_params(("arbitrary",), est),
        name="router",
    )(x2d, g2, rw_t)


def _count(mask_f32):
    return jnp.sum(jnp.sum(mask_f32, axis=0, keepdims=True), axis=1, keepdims=True)


def _route_kernel(lg_ref, lgt_ref, idx_ref, gate_ref, aff_ref, afft_ref, *, nblocks, cap):
    n_exp, nbp, _ = lg_ref.shape

    def softmax0(l):
        ex = jnp.exp(l - jnp.max(l, axis=0, keepdims=True))
        return ex / jnp.sum(ex, axis=0, keepdims=True)

    aff_ref[...] = softmax0(lg_ref[...])
    afft_ref[...] = softmax0(lgt_ref[...])

    f32 = lambda m: jnp.where(m, 1.0, 0.0).astype(_F32)
    bf = lambda x: x.astype(_BF16)
    mm = lambda a, b: jnp.dot(a, b, preferred_element_type=_F32)

    ii = lambda shape, dim: lax.broadcasted_iota(jnp.int32, shape, dim)
    u_incl = bf(f32(ii((_LANES, _LANES), 0) <= ii((_LANES, _LANES), 1)))
    l_incl = bf(f32(ii((_LANES, _LANES), 1) <= ii((_LANES, _LANES), 0)))
    l_strict = bf(f32(ii((nbp, nbp), 1) < ii((nbp, nbp), 0)))
    u_strict = bf(f32(ii((nbp, nbp), 0) < ii((nbp, nbp), 1)))
    ones_ll = jnp.ones((_LANES, _LANES), _BF16)
    ones_8l = jnp.ones((_SUBLANES, _LANES), _BF16)
    ones_8b = jnp.ones((_SUBLANES, nbp), _BF16)
    blk_a = ii((nbp, _LANES), 0)
    blk_t = ii((_LANES, nbp), 1)
    p_row = ii((1, cap), 1).astype(_F32)
    blk_p = ii((nbp, cap), 0).astype(_F32)
    lane_p = ii((_LANES, cap), 0).astype(_F32)

    def per_expert(e, carry):
        a = aff_ref[e]
        at = afft_ref[e]
        bits = jnp.where(blk_a < nblocks, pltpu.bitcast(a, jnp.int32), -1)
        bits_t = jnp.where(blk_t < nblocks, pltpu.bitcast(at, jnp.int32), -1)

        def bit_step(i, thr):
            cand = thr | lax.shift_left(jnp.int32(1), 30 - i)
            return jnp.where(_count(f32(bits >= cand)) >= cap, cand, thr)

        thr = lax.fori_loop(0, 31, bit_step, jnp.zeros((1, 1), jnp.int32))
        gt, tie = f32(bits > thr), f32(bits == thr)
        gt_t, tie_t = f32(bits_t > thr), f32(bits_t == thr)
        need = cap - _count(gt)

        tie_b = bf(tie)
        rank = mm(l_strict, bf(mm(tie_b, ones_ll))) + mm(tie_b, u_incl) - tie
        sel = bf(jnp.where(rank < need, tie, 0.0) + gt)
        tie_tb = bf(tie_t)
        rank_t = mm(bf(mm(ones_8l, tie_tb)), u_strict)[0:1, :] + mm(l_incl, tie_tb) - tie_t
        sel_t = bf(jnp.where(rank_t < need, tie_t, 0.0) + gt_t)

        tot_b = mm(sel, ones_ll)
        cend = mm(l_strict, bf(tot_b)) + tot_b
        cl_t = bf(mm(l_incl, sel_t))
        tot_row = bf(mm(ones_8l, sel_t))

        ind_le = bf(f32(cend[:, 0:1] <= p_row))
        blk_of_p = mm(ones_8b, ind_le)[0:1, :]
        p_local = p_row - mm(tot_row, ind_le)[0:1, :]
        onehot = bf(f32(blk_p == blk_of_p))
        cl_of_p = mm(cl_t, onehot)
        off = mm(ones_8l, bf(f32(cl_of_p <= p_local)))[0:1, :]
        idx_ref[e] = (blk_of_p * _LANES + off).astype(jnp.int32)

        hi = bf(at)
        r1 = at - hi.astype(_F32)
        mid = bf(r1)
        lo = bf(r1 - mid.astype(_F32))
        g_of_p = mm(hi, onehot) + mm(mid, onehot) + mm(lo, onehot)
        gate_ref[e] = jnp.sum(jnp.where(lane_p == off, g_of_p, 0.0), axis=0, keepdims=True)
        return carry

    lax.fori_loop(0, n_exp, per_expert, 0)


def _route(logits_t, *, cap):
    e, n = logits_t.shape
    nblocks = n // _LANES
    nbp = max(nblocks, _LANES)
    lg = logits_t.reshape(e, nblocks, _LANES)
    lg = jnp.pad(lg, ((0, 0), (0, nbp - nblocks), (0, 0)))
    lgt = jnp.swapaxes(lg, 1, 2)
    est = 6 * e * nbp * _LANES * 4 + 12 * max(nbp, _LANES) * cap * 4
    full = lambda shape: pl.BlockSpec(shape, lambda: (0,) * len(shape))
    idx, gate = pl.pallas_call(
        functools.partial(_route_kernel, nblocks=nblocks, cap=cap),
        in_specs=[full((e, nbp, _LANES)), full((e, _LANES, nbp))],
        out_specs=[full((e, 1, cap)), full((e, 1, cap))],
        out_shape=[jax.ShapeDtypeStruct((e, 1, cap), jnp.int32), jax.ShapeDtypeStruct((e, 1, cap), _F32)],
        scratch_shapes=[pltpu.VMEM((e, nbp, _LANES), _F32), pltpu.VMEM((e, _LANES, nbp), _F32)],
        compiler_params=pltpu.CompilerParams(vmem_limit_bytes=_vmem_limit(est)),
        name="route",
    )(lg, lgt)
    return idx.reshape(e * cap), gate.reshape(e * cap, 1)


def _ffn_up_kernel(idx_ref, x_hbm, g_ref, wg_ref, wu_ref, o_ref, xbuf, hn_ref, sem, *, tm):
    g, f = pl.program_id(0), pl.program_id(1)
    n_tiles, n_f = pl.num_programs(0), pl.num_programs(1)
    share = tm // n_f
    norm_chunk = min(256, tm)

    def row_copy(tile, r):
        return pltpu.make_async_copy(x_hbm.at[pl.ds(idx_ref[tile * tm + r], 1)], xbuf.at[pl.ds(r, 1)], sem)

    def all_rows(tile, go):
        def body(r, carry):
            cp = row_copy(tile, r)
            cp.start() if go else cp.wait()
            return carry

        lax.fori_loop(0, tm, body, 0, unroll=8)

    @pl.when(f == 0)
    def _():
        @pl.when(g == 0)
        def _():
            all_rows(0, True)

        all_rows(g, False)

        def norm_rows(c, carry):
            r0 = pl.multiple_of(c * norm_chunk, norm_chunk)
            hn_ref[pl.ds(r0, norm_chunk), :] = _rms_scale(xbuf[pl.ds(r0, norm_chunk), :], g_ref[...]).astype(_BF16)
            return carry

        lax.fori_loop(0, tm // norm_chunk, norm_rows, 0)

    nxt = jnp.minimum(g + 1, n_tiles - 1)
    for r in range(share):
        row_copy(nxt, f * share + r).start()

    hn = hn_ref[...]
    gate = jnp.dot(hn, wg_ref[...].astype(_BF16), preferred_element_type=_F32)
    up = jnp.dot(hn, wu_ref[...].astype(_BF16), preferred_element_type=_F32)
    o_ref[...] = (jax.nn.silu(gate) * up).astype(o_ref.dtype)

    @pl.when(jnp.logical_and(g == n_tiles - 1, f == n_f - 1))
    def _():
        all_rows(nxt, False)


def _ffn_up(idx, x2d, g2, w_gate, w_up, *, cap):
    slots = idx.shape[0]
    d = x2d.shape[1]
    ff = w_gate.shape[2]
    tm = min(1024, cap)
    tf = min(256, ff)
    per_e = cap // tm
    est = tm * d * 4 + tm * d * 2 + 4 * d * tf * 4 + 2 * d * tf * 2 + 2 * tm * tf * 2 + 3 * tm * tf * 4 + tm * d * 4
    return pl.pallas_call(
        functools.partial(_ffn_up_kernel, tm=tm),
        grid_spec=pltpu.PrefetchScalarGridSpec(
            num_scalar_prefetch=1,
            grid=(slots // tm, ff // tf),
            in_specs=[
                pl.BlockSpec(memory_space=pl.ANY),
                pl.BlockSpec((1, d), lambda g, f, idx_ref: (0, 0)),
                pl.BlockSpec((None, d, tf), lambda g, f, idx_ref: (g // per_e, 0, f)),
                pl.BlockSpec((None, d, tf), lambda g, f, idx_ref: (g // per_e, 0, f)),
            ],
            out_specs=pl.BlockSpec((tm, tf), lambda g, f, idx_ref: (g, f)),
            scratch_shapes=[pltpu.VMEM((tm, d), _F32), pltpu.VMEM((tm, d), _BF16), pltpu.SemaphoreType.DMA(())],
        ),
        out_shape=jax.ShapeDtypeStruct((slots, ff), _BF16),
        compiler_params=_params(("arbitrary", "arbitrary"), est),
        name="ffn_up",
    )(idx, x2d, g2, w_gate, w_up)


def _ffn_down_kernel(idx_ref, x_hbm, h_ref, w_ref, gate_ref, o_hbm, buf, sem_in, sem_out, *, rows, per_e):
    del x_hbm
    g = pl.program_id(0)
    slot = lax.rem(g, 2)
    pos = lax.rem(g, per_e)
    first = pos == 0
    last = pos == per_e - 1

    def rows_of(tile, buf_slot, fetch, go):
        base = tile * rows

        for r in range(rows):
            hbm_row = o_hbm.at[pl.ds(idx_ref[base + r], 1)]
            vmem_row = buf.at[buf_slot, pl.ds(r, 1)]
            cp = (pltpu.make_async_copy(hbm_row, vmem_row, sem_in.at[buf_slot]) if fetch
                  else pltpu.make_async_copy(vmem_row, hbm_row, sem_out.at[buf_slot]))
            cp.start() if go else cp.wait()

    @pl.when(first)
    def _():
        rows_of(g, slot, True, True)

    rows_of(g, slot, True, False)

    @pl.when(jnp.logical_not(first))
    def _():
        rows_of(g - 1, 1 - slot, False, False)

    @pl.when(jnp.logical_not(last))
    def _():
        rows_of(g + 1, 1 - slot, True, True)

    y = jnp.dot(h_ref[...], w_ref[...], preferred_element_type=_F32) * gate_ref[...]
    buf[slot] = buf[slot] + y
    rows_of(g, slot, False, True)

    @pl.when(last)
    def _():
        rows_of(g, slot, False, False)


def _ffn_down(idx, x2d, hid, w_down, gates, *, cap):
    slots, ff = hid.shape
    d = w_down.shape[2]
    rows = min(256, cap)
    assert cap % rows == 0
    per_e = cap // rows
    est = 2 * ff * d * 2 + 2 * rows * ff * 2 + 2 * rows * d * 4 + 2 * rows * d * 4 + 2 * rows * _LANES * 4
    return pl.pallas_call(
        functools.partial(_ffn_down_kernel, rows=rows, per_e=per_e),
        grid_spec=pltpu.PrefetchScalarGridSpec(
            num_scalar_prefetch=1,
            grid=(slots // rows,),
            in_specs=[pl.BlockSpec(memory_space=pl.ANY),
                      pl.BlockSpec((rows, ff), lambda g, idx_ref: (g, 0)),
                      pl.BlockSpec((None, ff, d), lambda g, idx_ref: (g // per_e, 0, 0)),
                      pl.BlockSpec((rows, 1), lambda g, idx_ref: (g, 0))],
            out_specs=pl.BlockSpec(memory_space=pl.ANY),
            scratch_shapes=[pltpu.VMEM((2, rows, d), _F32), pltpu.SemaphoreType.DMA((2,)),
                            pltpu.SemaphoreType.DMA((2,))],
        ),
        out_shape=jax.ShapeDtypeStruct(x2d.shape, x2d.dtype),
        input_output_aliases={1: 0},
        compiler_params=pltpu.CompilerParams(dimension_semantics=("arbitrary",), vmem_limit_bytes=_vmem_limit(est),
                                             has_side_effects=True),
        name="ffn_down",
    )(idx, x2d, hid, w_down, gates)


def _final_norm_kernel(x_ref, g_ref, o_ref):
    o_ref[...] = _rms_scale(x_ref[...], g_ref[...])


def _final_norm(x2d, g):
    n, d = x2d.shape
    tm = min(512, n)
    return pl.pallas_call(
        _final_norm_kernel,
        grid=(n // tm,),
        in_specs=[pl.BlockSpec((tm, d), lambda i: (i, 0)), pl.BlockSpec((1, d), lambda i: (0, 0))],
        out_specs=pl.BlockSpec((tm, d), lambda i: (i, 0)),
        out_shape=jax.ShapeDtypeStruct((n, d), _F32),
        compiler_params=_params(("arbitrary",), 5 * tm * d * 4),
        name="final_norm",
    )(x2d, g)


def _rope_tables(seq):
    half = _ROT_DIM // 2
    inv_freq = _ROPE_THETA ** (-jnp.arange(0, _ROT_DIM, 2, dtype=_F32) / _ROT_DIM)
    ang = jnp.arange(seq, dtype=_F32)[:, None] * inv_freq[None, :]
    cos, sin = jnp.cos(ang), jnp.sin(ang)
    zeros = lambda k: jnp.zeros((seq, k), _F32)
    c = jnp.concatenate([cos, cos, jnp.ones((seq, _LANES - _ROT_DIM), _F32)], axis=1)
    sa = jnp.concatenate([-sin, zeros(_LANES - half)], axis=1)
    sb = jnp.concatenate([zeros(half), sin, zeros(_LANES - _ROT_DIM)], axis=1)
    q_scale = _HEAD_DIM ** -0.5 * math.log2(math.e)
    stack = lambda t: jnp.stack([t * q_scale, t])
    return stack(c), stack(sa), stack(sb)


def _trunk(x, p):
    batch, seq, d = x.shape
    n = batch * seq
    lru_w = p["conv_w"].shape[1]
    att_w = (p["w_in"].shape[1] - 2 * lru_w) // 3
    n_exp = p["rw_t"].shape[0]
    cap = max(1, _EC_CAPACITY_FACTOR * n // n_exp)
    x2d = x.reshape(n, d)

    u = _in_proj(x2d, p["norm1_g"], p["w_in"], *p["rope"], seq=seq, lru_w=lru_w, att_w=att_w)
    y_rec = _rglru(u, p["conv_w"], p["conv_b"], p["lru_wa"], p["lru_ba"], p["lru_wi"], p["lru_bi"], p["lru_L"],
                   p["lru_norm_g"], batch=batch, seq=seq, lru_w=lru_w)
    y_att = _diff_attn(u, p["diff_lambda"], p["subln_g"], batch=batch, seq=seq, lru_w=lru_w, att_w=att_w)
    x1 = _out_proj(x2d, y_rec, y_att, p["w_out"])

    logits_t = _router_logits(x1, p["norm2_g"], p["rw_t"])
    idx, gates = _route(logits_t, cap=cap)
    hid = _ffn_up(idx, x1, p["norm2_g"], p["w_gate"], p["w_up"], cap=cap)
    x2 = _ffn_down(idx, x1, hid, p["w_down"], gates, cap=cap)
    return _final_norm(x2, p["final_g"]).reshape(batch, seq, d)


def kernel(x_prompt, x_sample, norm1_g, w_in, conv_w, conv_b, lru_wa, lru_ba, lru_wi, lru_bi, lru_L, lru_norm_g,
           diff_lambda, subln_g, w_out, norm2_g, router_w, w_gate, w_up, w_down, final_g):
    assert norm1_g.shape[0] == 1, "single-layer trunk"
    row = lambda v: v.reshape(1, -1).astype(_F32)
    p = {
        "norm1_g": row(norm1_g[0]), "w_in": w_in[0].astype(_BF16),
        "conv_w": conv_w[0], "conv_b": row(conv_b[0]),
        "lru_wa": (0.5 * lru_wa[0]).astype(_BF16), "lru_ba": lru_ba[0], "lru_wi": (0.5 * lru_wi[0]).astype(_BF16),
        "lru_bi": lru_bi[0], "lru_L": lru_L[0], "lru_norm_g": row(lru_norm_g[0]),
        "diff_lambda": diff_lambda[0], "subln_g": row(subln_g[0]),
        "w_out": w_out[0].astype(_BF16), "norm2_g": row(norm2_g[0]),
        "rw_t": router_w[0].T.astype(_BF16),
        "w_gate": w_gate[0], "w_up": w_up[0], "w_down": w_down[0].astype(_BF16),
        "final_g": row(final_g),
        "rope": _rope_tables(x_prompt.shape[1]),
    }
    assert x_prompt.shape[1] == x_sample.shape[1], "both groups share the rotary tables"
    return _trunk(x_prompt, p), _trunk(x_sample, p)
```
